```python
import jax, jax.numpy as jnp
from jax import lax
import numpy as np

D_MODEL = 2048
BATCH = 1
SEQ = 8192
DEPTH = 2
DEC_BATCH = 8
DEC_SEQ = 2048
PAST_LEN = 128

GRID_W = 64
N_META = 16
ATTN_HEADS = 16
HEAD_DIM = 64
ATTN_WIDTH = ATTN_HEADS * HEAD_DIM
POOL_WIDTH = D_MODEL - ATTN_WIDTH
POOL_WINDOWS = (2, 4, 8, 16)
N_POOL_GROUPS = len(POOL_WINDOWS)
POOL_GROUP = POOL_WIDTH // N_POOL_GROUPS
IN_WIDTH = 3 * ATTN_WIDTH + POOL_WIDTH
NA_ROWS_MAX = 8
NA_COLS = 16
D_FF = 5632
N_EXPERTS = 8
TOP_K = 2
N_DENSE = (DEPTH + 1) // 2
N_MOE = DEPTH // 2
EPS = 1e-6

kernel_name = "hybrid_natten_pool_encoder"


def rmsnorm(x, g):
    xf = x.astype(jnp.float32)
    y = xf * lax.rsqrt(jnp.mean(xf * xf, axis=-1, keepdims=True) + EPS)
    return (y * g.astype(jnp.float32)).astype(x.dtype)


def swiglu(h, w_gate, w_up, w_down):
    return (jax.nn.silu(h @ w_gate) * (h @ w_up)) @ w_down


def neighbourhood_attention(q, k, v, rpb, meta_bias):
    B, L, H, Dh = q.shape
    T = L - N_META
    rows = T // GRID_W
    kh = min(NA_ROWS_MAX, rows)
    scale = Dh ** -0.5
    f32 = jnp.float32
    qm, km, vm = q[:, :N_META], k[:, :N_META], v[:, :N_META]
    qg = q[:, N_META:].reshape(B, rows, GRID_W, H, Dh)
    kg = k[:, N_META:].reshape(B, rows, GRID_W, H, Dh)
    vg = v[:, N_META:].reshape(B, rows, GRID_W, H, Dh)

    row_start = jnp.clip(jnp.arange(rows) - kh // 2, 0, rows - kh)
    col_pos = jnp.arange(GRID_W)
    col_start = jnp.clip(col_pos - NA_COLS // 2, 0, GRID_W - NA_COLS)
    col_idx = col_start[:, None] + jnp.arange(NA_COLS)
    col_off = col_idx - col_pos[:, None] + (NA_COLS - 1)
    rpb_cols = rpb.astype(f32)[:, :, col_off]
    meta_b = meta_bias.astype(f32)[:, None, :]

    def row_block(r):
        r0 = row_start[r]
        kb = lax.dynamic_slice_in_dim(kg, r0, kh, axis=1)
        vb = lax.dynamic_slice_in_dim(vg, r0, kh, axis=1)
        kw = kb[:, :, col_idx]
        vw = vb[:, :, col_idx]
        row_off = r0 + jnp.arange(kh) - r + (NA_ROWS_MAX - 1)
        bias = jnp.transpose(rpb_cols[:, row_off], (0, 2, 1, 3))
        qr = qg[:, r]
        s_loc = jnp.einsum('bqhd,bjqkhd->bhqjk', qr, kw).astype(f32) * scale + bias
        s_meta = jnp.einsum('bqhd,bmhd->bhqm', qr, km).astype(f32) * scale + meta_b
        s = jnp.concatenate([s_loc.reshape(B, H, GRID_W, kh * NA_COLS), s_meta], axis=-1)
        p = jax.nn.softmax(s, axis=-1).astype(v.dtype)
        p_loc = p[..., :kh * NA_COLS].reshape(B, H, GRID_W, kh, NA_COLS)
        p_meta = p[..., kh * NA_COLS:]
        return (jnp.einsum('bhqjk,bjqkhd->bqhd', p_loc, vw)
                + jnp.einsum('bhqm,bmhd->bqhd', p_meta, vm))

    o_grid = lax.map(row_block, jnp.arange(rows))
    o_grid = jnp.moveaxis(o_grid, 0, 1).reshape(B, T, H, Dh)
    s_mm = jnp.einsum('bqhd,bmhd->bhqm', qm, km).astype(f32) * scale + meta_b
    o_meta = jnp.einsum('bhqm,bmhd->bqhd', jax.nn.softmax(s_mm, axis=-1).astype(v.dtype), vm)
    return jnp.concatenate([o_meta, o_grid], axis=1)


def multiscale_pool(u, pool_w, pool_scale):
    B, L, C = u.shape
    f32 = jnp.float32
    uf = u.astype(f32).reshape(B, L, N_POOL_GROUPS, POOL_GROUP)
    cs = jnp.concatenate([jnp.zeros((B, 1, N_POOL_GROUPS, POOL_GROUP), f32),
                          jnp.cumsum(uf, axis=1)], axis=1)
    t = jnp.arange(L)
    means = []
    for g, w in enumerate(POOL_WINDOWS):
        lo = jnp.clip(t - w // 2, 0, L)
        hi = jnp.clip(t + w // 2, 0, L)
        cnt = (hi - lo).astype(f32)
        csg = cs[:, :, g]
        means.append((csg[:, hi] - csg[:, lo]) / cnt[None, :, None])
    pooled = (jnp.stack(means, axis=2) - uf).astype(u.dtype)
    mixed = jnp.einsum('blgc,gcd->blgd', pooled, pool_w)
    return mixed.reshape(B, L, C) * pool_scale


def token_mixer(h, w_in, rpb, meta_bias, pool_w, pool_scale, g_attn_out, g_pool_out, w_out):
    B, L, _ = h.shape
    z = h @ w_in
    q = z[..., :ATTN_WIDTH].reshape(B, L, ATTN_HEADS, HEAD_DIM)
    k = z[..., ATTN_WIDTH:2 * ATTN_WIDTH].reshape(B, L, ATTN_HEADS, HEAD_DIM)
    v = z[..., 2 * ATTN_WIDTH:3 * ATTN_WIDTH].reshape(B, L, ATTN_HEADS, HEAD_DIM)
    u = z[..., 3 * ATTN_WIDTH:]
    o_attn = neighbourhood_attention(q, k, v, rpb, meta_bias).reshape(B, L, ATTN_WIDTH)
    o_pool = multiscale_pool(u, pool_w, pool_scale)
    o = jnp.concatenate([rmsnorm(o_attn, g_attn_out), rmsnorm(o_pool, g_pool_out)], axis=-1)
    return o @ w_out


def moe_ffn(h, router, w_gate, w_up, w_down):
    B, L, D = h.shape
    hf = h.reshape(B * L, D)
    logits = (hf @ router).astype(jnp.float32)
    top_v, top_i = lax.top_k(logits, TOP_K)
    gates = jax.nn.softmax(top_v, axis=-1)
    comb = jnp.sum(jax.nn.one_hot(top_i, N_EXPERTS, dtype=jnp.float32) * gates[..., None], axis=1)
    comb = comb.astype(h.dtype)
    y = jnp.zeros_like(hf)
    for e in range(N_EXPERTS):
        y = y + comb[:, e:e + 1] * swiglu(hf, w_gate[e], w_up[e], w_down[e])
    return y.reshape(B, L, D)


def trunk(x, meta_tokens, ln_mix, w_in, rpb, meta_bias, pool_w, pool_scale, g_attn_out,
          g_pool_out, w_out, ln_ffn, ffn_w_gate, ffn_w_up, ffn_w_down, router, moe_w_gate,
          moe_w_up, moe_w_down, g_final):
    B = x.shape[0]
    meta = jnp.broadcast_to(meta_tokens[None].astype(x.dtype), (B, N_META, D_MODEL))
    h = jnp.concatenate([meta, x], axis=1)
    for layer in range(DEPTH):
        h = h + token_mixer(rmsnorm(h, ln_mix[layer]), w_in[layer], rpb[layer],
                            meta_bias[layer], pool_w[layer], pool_scale[layer],
                            g_attn_out[layer], g_pool_out[layer], w_out[layer])
        hn = rmsnorm(h, ln_ffn[layer])
        i = layer // 2
        if layer % 2 == 0:
            h = h + swiglu(hn, ffn_w_gate[i], ffn_w_up[i], ffn_w_down[i])
        else:
            h = h + moe_ffn(hn, router[i], moe_w_gate[i], moe_w_up[i], moe_w_down[i])
    return rmsnorm(h, g_final)[:, N_META:]


def setup_inputs(seed: int = 0) -> dict:
    key = jax.random.key(seed)
    ks = jax.random.split(key, 24)
    f32 = jnp.float32
    nrm = lambda k, shape, s: jax.random.normal(k, shape, f32) * s
    gain = lambda k, shape: 1.0 + 0.02 * jax.random.normal(k, shape, f32)
    return {
        "x_prompt": nrm(ks[0], (BATCH, SEQ, D_MODEL), 1.0),
        "x_sample": nrm(ks[1], (DEC_BATCH, DEC_SEQ, D_MODEL), 1.0),
        "meta_tokens": nrm(ks[2], (N_META, D_MODEL), 1.0),
        "ln_mix": gain(ks[3], (DEPTH, D_MODEL)),
        "w_in": nrm(ks[4], (DEPTH, D_MODEL, IN_WIDTH), D_MODEL ** -0.5),
        "rpb": nrm(ks[5], (DEPTH, ATTN_HEADS, 2 * NA_ROWS_MAX - 1, 2 * NA_COLS - 1), 0.1),
        "meta_bias": nrm(ks[6], (DEPTH, ATTN_HEADS, N_META), 0.1),
        "pool_w": nrm(ks[7], (DEPTH, N_POOL_GROUPS, POOL_GROUP, POOL_GROUP), POOL_GROUP ** -0.5),
        "pool_scale": gain(ks[8], (DEPTH, POOL_WIDTH)),
        "g_attn_out": gain(ks[9], (DEPTH, ATTN_WIDTH)),
        "g_pool_out": gain(ks[10], (DEPTH, POOL_WIDTH)),
        "w_out": nrm(ks[11], (DEPTH, D_MODEL, D_MODEL), D_MODEL ** -0.5),
        "ln_ffn": gain(ks[12], (DEPTH, D_MODEL)),
        "ffn_w_gate": nrm(ks[13], (N_DENSE, D_MODEL, D_FF), D_MODEL ** -0.5),
        "ffn_w_up": nrm(ks[14], (N_DENSE, D_MODEL, D_FF), D_MODEL ** -0.5),
        "ffn_w_down": nrm(ks[15], (N_DENSE, D_FF, D_MODEL), D_FF ** -0.5),
        "router": nrm(ks[16], (N_MOE, D_MODEL, N_EXPERTS), D_MODEL ** -0.5),
        "moe_w_gate": nrm(ks[17], (N_MOE, N_EXPERTS, D_MODEL, D_FF), D_MODEL ** -0.5),
        "moe_w_up": nrm(ks[18], (N_MOE, N_EXPERTS, D_MODEL, D_FF), D_MODEL ** -0.5),
        "moe_w_down": nrm(ks[19], (N_MOE, N_EXPERTS, D_FF, D_MODEL), D_FF ** -0.5),
        "g_final": gain(ks[20], (D_MODEL,)),
    }


def reference(x_prompt, x_sample, meta_tokens, ln_mix, w_in, rpb, meta_bias, pool_w, pool_scale,
              g_attn_out, g_pool_out, w_out, ln_ffn, ffn_w_gate, ffn_w_up, ffn_w_down, router,
              moe_w_gate, moe_w_up, moe_w_down, g_final):
    y_prompt = trunk(x_prompt, meta_tokens, ln_mix, w_in, rpb, meta_bias, pool_w, pool_scale,
                     g_attn_out, g_pool_out, w_out, ln_ffn, ffn_w_gate, ffn_w_up, ffn_w_down,
                     router, moe_w_gate, moe_w_up, moe_w_down, g_final)
    y_sample = trunk(x_sample, meta_tokens, ln_mix, w_in, rpb, meta_bias, pool_w, pool_scale,
                     g_attn_out, g_pool_out, w_out, ln_ffn, ffn_w_gate, ffn_w_up, ffn_w_down,
                     router, moe_w_gate, moe_w_up, moe_w_down, g_final)
    return (y_prompt, y_sample)
```

```python
import functools

import numpy as np
import jax
import jax.numpy as jnp
from jax import lax
from jax.experimental import pallas as pl
from jax.experimental.pallas import tpu as pltpu

F32 = jnp.float32
BF16 = jnp.bfloat16

D_MODEL = 2048
GRID_W = 64
N_META = 16
ATTN_HEADS = 16
HEAD_DIM = 64
ATTN_WIDTH = ATTN_HEADS * HEAD_DIM
POOL_WIDTH = D_MODEL - ATTN_WIDTH
POOL_WINDOWS = (2, 4, 8, 16)
POOL_GROUP = POOL_WIDTH // len(POOL_WINDOWS)
POOL_HALO = max(POOL_WINDOWS) // 2
NA_ROWS = 8
NA_COLS = 16
D_FF = 5632
N_EXPERTS = 8
EPS = 1e-6
NEG = -1e30

LANES = 128
HEAD_PAIR = LANES // HEAD_DIM
VMEM_LIMIT = 56 * 1024 * 1024

TM_IN = 1024
TN_IN = 1024
TM_MIX = 256
TM_FFN = 512
TF_FFN = 512
TM_OUT = 256
ATTN_CHUNK = 32
ROW_ALIGN = 1024
ROW_MULT = 16


def _round_up(x, m):
    return (x + m - 1) // m * m


def _rms(x, g):
    return x * lax.rsqrt(jnp.mean(x * x, axis=-1, keepdims=True) + EPS) * g


def _dot(a, b):
    return jnp.dot(a, b, preferred_element_type=F32)


def _dot_nt(a, b):
    return lax.dot_general(a, b, (((1,), (1,)), ((), ())), preferred_element_type=F32)


class _Layout:
    def __init__(self, groups):
        self.groups = groups
        self.seq_start, self.seq_tokens = [], []
        row = 0
        for b, t in groups:
            assert t % (GRID_W * NA_ROWS) == 0
            for _ in range(b):
                self.seq_start.append(row)
                self.seq_tokens.append(t)
                row += N_META + t
        self.n_valid = row
        self.rows = _round_up(row, ROW_ALIGN)
        pos = np.zeros((self.rows, 1), np.int32)
        rem = np.ones((self.rows, 1), np.int32)
        for s, t in zip(self.seq_start, self.seq_tokens):
            n = N_META + t
            pos[s:s + n, 0] = np.arange(n)
            rem[s:s + n, 0] = n - np.arange(n)
        self.pos, self.rem = pos, rem
        items = []
        self.chunk = ATTN_CHUNK
        self.kwin = self.chunk + NA_ROWS
        for s, t in zip(self.seq_start, self.seq_tokens):
            g = t // GRID_W
            assert g % self.chunk == 0
            for c in range(g // self.chunk):
                r_base = c * self.chunk
                kv_row0 = int(np.clip(r_base - NA_ROWS // 2, 0, max(g - self.kwin, 0)))
                items.append((s + N_META + r_base * GRID_W, s + N_META + kv_row0 * GRID_W,
                              s, r_base, g, kv_row0))
        self.items = np.asarray(items, np.int32)
        assert int(self.items[:, 1].max()) + self.kwin * GRID_W <= self.rows
        self.td = max(d for d in range(8, 513, 8) if self.n_valid % d == 0)
        self.slots = _round_up(2 * self.n_valid + N_EXPERTS * (TM_FFN - 1), TM_FFN)


def _in_proj_kernel(x_ref, g_ref, w_ref, qkv_ref, u_ref, xn_ref):
    j = pl.program_id(1)

    @pl.when(j == 0)
    def _():
        xn_ref[...] = _rms(x_ref[...], g_ref[...]).astype(BF16)

    z = _dot(xn_ref[...], w_ref[...])

    @pl.when(j == 0)
    def _():
        qkv_ref[...] = (z * (HEAD_DIM ** -0.5)).astype(BF16)

    @pl.when(jnp.logical_and(j > 0, j < 3))
    def _():
        qkv_ref[...] = z.astype(BF16)

    @pl.when(j == 3)
    def _():
        u_ref[...] = z


def _in_proj(h, g, w_bf):
    rows = h.shape[0]
    assert TN_IN == ATTN_WIDTH == POOL_WIDTH
    return pl.pallas_call(
        _in_proj_kernel,
        grid=(rows // TM_IN, 4),
        in_specs=[
            pl.BlockSpec((TM_IN, D_MODEL), lambda i, j: (i, 0)),
            pl.BlockSpec((1, D_MODEL), lambda i, j: (0, 0)),
            pl.BlockSpec((D_MODEL, TN_IN), lambda i, j: (0, j)),
        ],
        out_specs=[
            pl.BlockSpec((TM_IN, TN_IN), lambda i, j: (i, jnp.minimum(j, 2))),
            pl.BlockSpec((TM_IN, TN_IN), lambda i, j: (i, 0)),
        ],
        out_shape=[
            jax.ShapeDtypeStruct((rows, 3 * ATTN_WIDTH), BF16),
            jax.ShapeDtypeStruct((rows, POOL_WIDTH), F32),
        ],
        scratch_shapes=[pltpu.VMEM((TM_IN, D_MODEL), BF16)],
        compiler_params=pltpu.CompilerParams(
            dimension_semantics=("arbitrary", "arbitrary"), vmem_limit_bytes=VMEM_LIMIT),
        name="in_proj",
    )(h, g.reshape(1, D_MODEL), w_bf)


def _attn_kernel(qoff_ref, kvoff_ref, moff_ref, rbase_ref, gsz_ref, kvrow_ref,
                 q_ref, k_ref, v_ref, km_ref, vm_ref, bias_ref, mb_ref, o_ref,
                 ka_ref, kb_ref, kme_ref, vme_ref, *, chunk):
    it = pl.program_id(0)
    r_base = rbase_ref[it]
    g_rows = gsz_ref[it]
    kv_row0 = kvrow_ref[it]
    is_a = lax.broadcasted_iota(jnp.int32, (1, LANES), 1) < HEAD_DIM

    k = k_ref[...]
    kz = jnp.zeros_like(k)
    ka_ref[...] = jnp.where(is_a, k, kz)
    kb_ref[...] = jnp.where(is_a, kz, k)
    km = km_ref[...]
    kmz = jnp.zeros_like(km)
    kme_ref[...] = jnp.zeros_like(kme_ref)
    kme_ref[0:N_META, :] = jnp.where(is_a, km, kmz)
    kme_ref[N_META:2 * N_META, :] = jnp.where(is_a, kmz, km)
    vme_ref[...] = jnp.zeros_like(vme_ref)
    vme_ref[0:N_META, :] = vm_ref[...]
    vme_ref[N_META:2 * N_META, :] = vm_ref[...]

    nk = NA_ROWS * GRID_W

    def body(r, carry):
        rg = r_base + r
        rs = jnp.clip(rg - NA_ROWS // 2, 0, g_rows - NA_ROWS)
        var = rs - rg + (NA_ROWS - 1)
        kl = pl.multiple_of((rs - kv_row0) * GRID_W, GRID_W)
        q0 = pl.multiple_of(r * GRID_W, GRID_W)
        q = q_ref[pl.ds(q0, GRID_W), :]
        vv = v_ref[pl.ds(kl, nk), :]
        s_meta = _dot_nt(q, kme_ref[...])
        outs = []
        for hd, kref in enumerate((ka_ref, kb_ref)):
            s = _dot_nt(q, kref[pl.ds(kl, nk), :]) + bias_ref[hd, var]
            sm = s_meta + mb_ref[hd]
            m = jnp.maximum(jnp.max(s, axis=1, keepdims=True), jnp.max(sm, axis=1, keepdims=True))
            p = jnp.exp(s - m)
            pm = jnp.exp(sm - m)
            l = jnp.sum(p, axis=1, keepdims=True) + jnp.sum(pm, axis=1, keepdims=True)
            o = _dot(p.astype(BF16), vv) + _dot(pm.astype(BF16), vme_ref[...])
            outs.append(o / l)
        o_ref[pl.ds(q0, GRID_W), :] = jnp.where(is_a, outs[0], outs[1]).astype(o_ref.dtype)
        return carry

    lax.fori_loop(0, chunk, body, 0, unroll=2)


def _attention_grid(qkv, o_init, bias8, mbias, lay):
    rows = qkv.shape[0]
    qrows = lay.chunk * GRID_W
    krows = lay.kwin * GRID_W
    n_items = lay.items.shape[0]
    n_pairs = ATTN_HEADS // HEAD_PAIR
    tabs = [jnp.asarray(lay.items[:, c]) for c in range(6)]
    el = pl.Element

    def qmap(it, hp, qo, ko, mo, rb, gs, kr):
        return (pl.multiple_of(qo[it], ROW_MULT), pl.multiple_of(hp * LANES, LANES))

    def kmap(col0):
        def f(it, hp, qo, ko, mo, rb, gs, kr):
            return (pl.multiple_of(ko[it], ROW_MULT), pl.multiple_of(col0 + hp * LANES, LANES))
        return f

    def mmap(col0):
        def f(it, hp, qo, ko, mo, rb, gs, kr):
            return (pl.multiple_of(mo[it], ROW_MULT), pl.multiple_of(col0 + hp * LANES, LANES))
        return f

    grid_spec = pltpu.PrefetchScalarGridSpec(
        num_scalar_prefetch=6,
        grid=(n_items, n_pairs),
        in_specs=[
            pl.BlockSpec((el(qrows), el(LANES)), qmap),
            pl.BlockSpec((el(krows), el(LANES)), kmap(ATTN_WIDTH)),
            pl.BlockSpec((el(krows), el(LANES)), kmap(2 * ATTN_WIDTH)),
            pl.BlockSpec((el(N_META), el(LANES)), mmap(ATTN_WIDTH)),
            pl.BlockSpec((el(N_META), el(LANES)), mmap(2 * ATTN_WIDTH)),
            pl.BlockSpec((HEAD_PAIR, NA_ROWS, GRID_W, NA_ROWS * GRID_W),
                         lambda it, hp, *_: (hp, 0, 0, 0)),
            pl.BlockSpec((HEAD_PAIR, 1, LANES), lambda it, hp, *_: (hp, 0, 0)),
            pl.BlockSpec(memory_space=pl.ANY),
        ],
        out_specs=pl.BlockSpec((el(qrows), el(LANES)), qmap),
        scratch_shapes=[
            pltpu.VMEM((krows, LANES), BF16),
            pltpu.VMEM((krows, LANES), BF16),
            pltpu.VMEM((LANES, LANES), BF16),
            pltpu.VMEM((LANES, LANES), BF16),
        ],
    )

    def kern(qo, ko, mo, rb, gs, kr, q, k, v, km, vm, b, mb, o_in, o, *scr):
        del o_in
        _attn_kernel(qo, ko, mo, rb, gs, kr, q, k, v, km, vm, b, mb, o, *scr, chunk=lay.chunk)

    return pl.pallas_call(
        kern,
        grid_spec=grid_spec,
        out_shape=jax.ShapeDtypeStruct((rows, ATTN_WIDTH), BF16),
        input_output_aliases={13: 0},
        compiler_params=pltpu.CompilerParams(
            dimension_semantics=("arbitrary", "arbitrary"), vmem_limit_bytes=VMEM_LIMIT),
        name="attn_grid",
    )(*tabs, qkv, qkv, qkv, qkv, qkv, bias8, mbias, o_init)


def _attn_meta_kernel(moff_ref, x_ref, mb_ref, o_in_ref, o_ref):
    del moff_ref, o_in_ref
    x = x_ref[...].astype(F32)
    outs = []
    for h in range(ATTN_HEADS):
        q = x[:, h * HEAD_DIM:(h + 1) * HEAD_DIM].astype(BF16)
        k = x[:, ATTN_WIDTH + h * HEAD_DIM:ATTN_WIDTH + (h + 1) * HEAD_DIM].astype(BF16)
        v = x[:, 2 * ATTN_WIDTH + h * HEAD_DIM:2 * ATTN_WIDTH + (h + 1) * HEAD_DIM].astype(BF16)
        s = _dot_nt(q, k) + mb_ref[h:h + 1, :]
        m = jnp.max(s, axis=1, keepdims=True)
        p = jnp.exp(s - m)
        l = jnp.sum(p, axis=1, keepdims=True)
        outs.append(_dot(p.astype(BF16), v) / l)
    o_ref[...] = jnp.concatenate(outs, axis=1).astype(o_ref.dtype)


def _attention_meta(qkv, o_init, meta_bias, lay):
    rows = qkv.shape[0]
    n_seq = len(lay.seq_start)
    moff = jnp.asarray(np.asarray(lay.seq_start, np.int32))
    el = pl.Element
    grid_spec = pltpu.PrefetchScalarGridSpec(
        num_scalar_prefetch=1,
        grid=(n_seq,),
        in_specs=[
            pl.BlockSpec((el(N_META), el(3 * ATTN_WIDTH)),
                         lambda b, mo: (pl.multiple_of(mo[b], ROW_MULT), 0)),
            pl.BlockSpec((ATTN_HEADS, N_META), lambda b, mo: (0, 0)),
            pl.BlockSpec(memory_space=pl.ANY),
        ],
        out_specs=pl.BlockSpec((el(N_META), el(ATTN_WIDTH)),
                               lambda b, mo: (pl.multiple_of(mo[b], ROW_MULT), 0)),
    )
    return pl.pallas_call(
        _attn_meta_kernel,
        grid_spec=grid_spec,
        out_shape=jax.ShapeDtypeStruct((rows, ATTN_WIDTH), BF16),
        input_output_aliases={3: 0},
        compiler_params=pltpu.CompilerParams(dimension_semantics=("arbitrary",)),
        name="attn_meta",
    )(moff, qkv, meta_bias, o_init)


def _attn_bias_tables(rpb, meta_bias):
    v = np.arange(NA_ROWS)
    ridx = v[:, None] + v[None, :]
    cq = np.arange(GRID_W)
    cs = np.clip(cq - NA_COLS // 2, 0, GRID_W - NA_COLS)
    ck = np.arange(GRID_W)
    valid = (ck[None, :] >= cs[:, None]) & (ck[None, :] < cs[:, None] + NA_COLS)
    cidx = np.clip(ck[None, :] - cq[:, None] + NA_COLS - 1, 0, 2 * NA_COLS - 2)
    b = rpb.astype(F32)[:, ridx[:, None, :, None], cidx[None, :, None, :]]
    b = jnp.where(jnp.asarray(valid)[None, None, :, None, :], b, NEG)
    bias8 = b.reshape(ATTN_HEADS, NA_ROWS, GRID_W, NA_ROWS * GRID_W)
    lane = np.arange(LANES)
    head = np.arange(ATTN_HEADS)
    slot = (head % HEAD_PAIR) * N_META
    src = np.clip(lane[None, :] - slot[:, None], 0, N_META - 1)
    ok = (lane[None, :] >= slot[:, None]) & (lane[None, :] < slot[:, None] + N_META)
    mb = jnp.where(jnp.asarray(ok), jnp.take_along_axis(meta_bias.astype(F32), jnp.asarray(src), 1), NEG)
    return bias8, mb.reshape(ATTN_HEADS, 1, LANES)


def _mix_out_kernel(*refs, moe, n_valid):
    if moe:
        (oa_ref, u_ref, up_ref, un_ref, pos_ref, rem_ref, pw_ref, ps_ref, ga_ref, gp_ref, wo_ref,
         h_ref, lf_ref, rt_ref, tri_ref, h1_ref, hn_ref, route_ref, cnt_ref, ext_ref, run_ref) = refs
    else:
        (oa_ref, u_ref, up_ref, un_ref, pos_ref, rem_ref, pw_ref, ps_ref, ga_ref, gp_ref, wo_ref,
         h_ref, lf_ref, h1_ref, hn_ref, ext_ref) = refs
    tm = u_ref.shape[0]
    hal = POOL_HALO
    ext_ref[0:hal, :] = up_ref[...]
    ext_ref[hal:hal + tm, :] = u_ref[...]
    ext_ref[hal + tm:hal + tm + hal, :] = un_ref[...]
    pos = pos_ref[...]
    rem = rem_ref[...]

    mixed = []
    for g, w in enumerate(POOL_WINDOWS):
        half = w // 2
        c0 = g * POOL_GROUP
        acc = ext_ref[hal:hal + tm, c0:c0 + POOL_GROUP]
        centre = acc
        for kk in range(-half, half):
            if kk == 0:
                continue
            sl = ext_ref[hal + kk:hal + kk + tm, c0:c0 + POOL_GROUP]
            ok = (pos >= -kk) if kk < 0 else (rem > kk)
            acc = acc + jnp.where(ok, sl, 0.0)
        cnt = (jnp.minimum(pos, half) + jnp.minimum(rem, half)).astype(F32)
        pooled = (acc / cnt - centre).astype(BF16)
        mixed.append(_dot(pooled, pw_ref[g]))
    o_pool = jnp.concatenate(mixed, axis=1) * ps_ref[...]
    n_pool = _rms(o_pool, gp_ref[...]).astype(BF16)
    n_attn = _rms(oa_ref[...].astype(F32), ga_ref[...]).astype(BF16)
    mix = _dot(n_attn, wo_ref[0:ATTN_WIDTH, :]) + _dot(n_pool, wo_ref[ATTN_WIDTH:D_MODEL, :])
    h1 = h_ref[...] + mix
    h1_ref[...] = h1
    hn = _rms(h1, lf_ref[...])
    if not moe:
        hn_ref[...] = hn.astype(BF16)
        return

    hn_ref[...] = hn
    i = pl.program_id(0)

    @pl.when(i == 0)
    def _():
        run_ref[...] = jnp.zeros_like(run_ref)

    hi = hn.astype(BF16)
    lo = (hn - hi.astype(F32)).astype(BF16)
    l1 = _dot_nt(rt_ref[...], hi)
    l2 = _dot_nt(rt_ref[0:16, :], lo)
    lg = l1[0:N_EXPERTS] + l1[16:16 + N_EXPERTS] + l2[0:N_EXPERTS]
    eidx = lax.broadcasted_iota(jnp.int32, (N_EXPERTS, tm), 0).astype(F32)
    none = float(N_EXPERTS)
    m1 = jnp.max(lg, axis=0, keepdims=True)
    i1 = jnp.min(jnp.where(lg == m1, eidx, none), axis=0, keepdims=True)
    sel1 = eidx == i1
    lg2 = jnp.where(sel1, -jnp.inf, lg)
    m2 = jnp.max(lg2, axis=0, keepdims=True)
    i2 = jnp.min(jnp.where(lg2 == m2, eidx, none), axis=0, keepdims=True)
    sel2 = eidx == i2
    t = jnp.exp(m2 - m1)
    g1 = 1.0 / (1.0 + t)
    g2 = t / (1.0 + t)
    rowid = i * tm + lax.broadcasted_iota(jnp.int32, (1, tm), 1)
    valid = rowid < n_valid
    c = jnp.where(jnp.logical_and(jnp.logical_or(sel1, sel2), valid), 1.0, 0.0)
    c16 = jnp.concatenate([c, jnp.zeros_like(c)], axis=0).astype(BF16)
    cs = _dot(c16, tri_ref[...])[0:N_EXPERTS]
    rank_all = run_ref[:, 0:1] + cs - 1.0
    rank1 = jnp.sum(jnp.where(sel1, rank_all, 0.0), axis=0, keepdims=True)
    rank2 = jnp.sum(jnp.where(sel2, rank_all, 0.0), axis=0, keepdims=True)
    run_new = run_ref[...] + jnp.sum(c, axis=1, keepdims=True)
    run_ref[...] = run_new
    cnt_ref[...] = run_new
    route = jnp.zeros((N_EXPERTS, tm), F32)
    for k, row in enumerate((i1, i2, g1, g2, rank1, rank2)):
        route = jnp.where(eidx == float(k), row, route)
    route_ref[...] = route


def _mix_out(o_attn, u, h, lay, pool_w_bf, pool_scale, g_attn, g_pool, w_out_bf, ln_ffn,
             router_t=None):
    rows = h.shape[0]
    tm = TM_MIX
    moe = router_t is not None
    blk8 = tm // POOL_HALO
    n8 = rows // POOL_HALO
    row1 = lambda a: a.reshape(1, -1)
    in_specs = [
        pl.BlockSpec((tm, ATTN_WIDTH), lambda i: (i, 0)),
        pl.BlockSpec((tm, POOL_WIDTH), lambda i: (i, 0)),
        pl.BlockSpec((POOL_HALO, POOL_WIDTH), lambda i: (jnp.maximum(i * blk8 - 1, 0), 0)),
        pl.BlockSpec((POOL_HALO, POOL_WIDTH), lambda i: (jnp.minimum((i + 1) * blk8, n8 - 1), 0)),
        pl.BlockSpec((tm, 1), lambda i: (i, 0)),
        pl.BlockSpec((tm, 1), lambda i: (i, 0)),
        pl.BlockSpec((len(POOL_WINDOWS), POOL_GROUP, POOL_GROUP), lambda i: (0, 0, 0)),
        pl.BlockSpec((1, POOL_WIDTH), lambda i: (0, 0)),
        pl.BlockSpec((1, ATTN_WIDTH), lambda i: (0, 0)),
        pl.BlockSpec((1, POOL_WIDTH), lambda i: (0, 0)),
        pl.BlockSpec((D_MODEL, D_MODEL), lambda i: (0, 0), pipeline_mode=pl.Buffered(1)),
        pl.BlockSpec((tm, D_MODEL), lambda i: (i, 0)),
        pl.BlockSpec((1, D_MODEL), lambda i: (0, 0)),
    ]
    args = [o_attn, u, u, u, jnp.asarray(lay.pos), jnp.asarray(lay.rem), pool_w_bf,
            row1(pool_scale), row1(g_attn), row1(g_pool), w_out_bf, h, row1(ln_ffn)]
    out_specs = [pl.BlockSpec((tm, D_MODEL), lambda i: (i, 0)),
                 pl.BlockSpec((tm, D_MODEL), lambda i: (i, 0))]
    out_shape = [jax.ShapeDtypeStruct((rows, D_MODEL), F32),
                 jax.ShapeDtypeStruct((rows, D_MODEL), F32 if moe else BF16)]
    scratch = [pltpu.VMEM((tm + 2 * POOL_HALO, POOL_WIDTH), F32)]
    if moe:
        tri = jnp.asarray(np.triu(np.ones((tm, tm), np.float32)), BF16)
        in_specs += [pl.BlockSpec((32, D_MODEL), lambda i: (0, 0)),
                     pl.BlockSpec((tm, tm), lambda i: (0, 0))]
        args += [router_t, tri]
        out_specs += [pl.BlockSpec((N_EXPERTS, tm), lambda i: (0, i)),
                      pl.BlockSpec((N_EXPERTS, LANES), lambda i: (0, 0))]
        out_shape += [jax.ShapeDtypeStruct((N_EXPERTS, rows), F32),
                      jax.ShapeDtypeStruct((N_EXPERTS, LANES), F32)]
        scratch += [pltpu.VMEM((N_EXPERTS, LANES), F32)]
    return pl.pallas_call(
        functools.partial(_mix_out_kernel, moe=moe, n_valid=lay.n_valid),
        grid=(rows // tm,),
        in_specs=in_specs,
        out_specs=out_specs,
        out_shape=out_shape,
        scratch_shapes=scratch,
        compiler_params=pltpu.CompilerParams(
            dimension_semantics=("arbitrary",), vmem_limit_bytes=VMEM_LIMIT),
        name="mix_out_moe" if moe else "mix_out",
    )(*args)


def _ffn_kernel(te_ref, nu_ref, *refs, moe):
    del te_ref
    if moe:
        x_ref, wg_ref, wu_ref, wd_ref, o_ref, xb_ref = refs
    else:
        x_ref, wg_ref, wu_ref, wd_ref, res_ref, o_ref = refs
    i = pl.program_id(0)
    j = pl.program_id(1)
    used = i < nu_ref[0]

    @pl.when(jnp.logical_and(jnp.logical_not(used), j == 0))
    def _():
        o_ref[...] = jnp.zeros_like(o_ref)

    @pl.when(used)
    def _():
        if moe:
            @pl.when(j == 0)
            def _():
                xb_ref[...] = x_ref[...].astype(BF16)
            x = xb_ref[...]
        else:
            x = x_ref[...]
        gate = _dot(x, wg_ref[...])
        up = _dot(x, wu_ref[...])
        mid = (gate / (1.0 + jnp.exp(-gate))) * up
        y = _dot(mid.astype(BF16), wd_ref[...])

        @pl.when(j == 0)
        def _():
            if moe:
                o_ref[...] = y
            else:
                o_ref[...] = res_ref[...] + y

        @pl.when(j > 0)
        def _():
            o_ref[...] += y


def _ffn(x, wg_bf, wu_bf, wd_bf, tile_expert, n_used, res=None):
    rows = x.shape[0]
    tm, tf = TM_FFN, TF_FFN
    nf = D_FF // tf
    moe = res is None

    def jj(i, j, nu):
        return jnp.where(i < nu[0], j, nf - 1)

    in_specs = [
        pl.BlockSpec((tm, D_MODEL), lambda i, j, te, nu: (i, 0)),
        pl.BlockSpec((None, D_MODEL, tf), lambda i, j, te, nu: (te[i], 0, jj(i, j, nu))),
        pl.BlockSpec((None, D_MODEL, tf), lambda i, j, te, nu: (te[i], 0, jj(i, j, nu))),
        pl.BlockSpec((None, tf, D_MODEL), lambda i, j, te, nu: (te[i], jj(i, j, nu), 0)),
    ]
    args = [x, wg_bf, wu_bf, wd_bf]
    scratch = []
    if moe:
        scratch = [pltpu.VMEM((tm, D_MODEL), BF16)]
    else:
        in_specs.append(pl.BlockSpec((tm, D_MODEL), lambda i, j, te, nu: (i, 0)))
        args.append(res)
    grid_spec = pltpu.PrefetchScalarGridSpec(
        num_scalar_prefetch=2,
        grid=(rows // tm, nf),
        in_specs=in_specs,
        out_specs=pl.BlockSpec((tm, D_MODEL), lambda i, j, te, nu: (i, 0)),
        scratch_shapes=scratch,
    )
    return pl.pallas_call(
        functools.partial(_ffn_kernel, moe=moe),
        grid_spec=grid_spec,
        out_shape=jax.ShapeDtypeStruct((rows, D_MODEL), F32),
        compiler_params=pltpu.CompilerParams(
            dimension_semantics=("arbitrary", "arbitrary"), vmem_limit_bytes=VMEM_LIMIT),
        name="ffn_moe" if moe else "ffn_dense",
    )(tile_expert, n_used, *args)


def _row_copy(src_ref, src_row, dst_ref, dst_row, sem):
    return pltpu.make_async_copy(src_ref.at[pl.ds(src_row, 1)], dst_ref.at[pl.ds(dst_row, 1)], sem)


def _dispatch_kernel(zstart_ref, zcount_ref, nu_ref, pos_ref, hn_ref, zero_ref, xs_ref, sem, zsem,
                     *, td, n_tiles):
    i = pl.program_id(0)

    @pl.when(i == 0)
    def _():
        def tile_copy(t):
            rows = pl.ds(pl.multiple_of(t * TM_FFN, TM_FFN), TM_FFN)
            return pltpu.make_async_copy(zero_ref, xs_ref.at[rows], zsem)

        def fill_tile(t, c):
            tile_copy(t).start()
            return c

        def drain_tile(t, c):
            tile_copy(t).wait()
            return c

        lax.fori_loop(nu_ref[0], n_tiles, fill_tile, 0)
        lax.fori_loop(nu_ref[0], n_tiles, drain_tile, 0)
        for e in range(N_EXPERTS):
            z0 = zstart_ref[e]
            n = zcount_ref[e]

            def fill(s, c):
                _row_copy(zero_ref, 0, xs_ref, z0 + s, zsem).start()
                return c

            def drain(s, c):
                _row_copy(zero_ref, 0, xs_ref, z0 + s, zsem).wait()
                return c

            lax.fori_loop(0, n, fill, 0)
            lax.fori_loop(0, n, drain, 0)

    base = i * td

    def issue(r, c):
        _row_copy(hn_ref, base + r, xs_ref, pos_ref[0, r], sem).start()
        _row_copy(hn_ref, base + r, xs_ref, pos_ref[1, r], sem).start()
        return c

    lax.fori_loop(0, td, issue, 0)
    for _ in range(2):
        pltpu.make_async_copy(hn_ref.at[pl.ds(0, td)], xs_ref.at[pl.ds(0, td)], sem).wait()


def _dispatch(hn, pos3, zstart, zcount, n_used, lay):
    td = lay.td
    n_steps = lay.n_valid // td
    zero = jnp.zeros((TM_FFN, D_MODEL), F32)
    grid_spec = pltpu.PrefetchScalarGridSpec(
        num_scalar_prefetch=3,
        grid=(n_steps,),
        in_specs=[
            pl.BlockSpec((None, 2, td), lambda i, zs, zc, nu: (i, 0, 0), memory_space=pltpu.SMEM),
            pl.BlockSpec(memory_space=pl.ANY),
            pl.BlockSpec(memory_space=pl.ANY),
        ],
        out_specs=pl.BlockSpec(memory_space=pl.ANY),
        scratch_shapes=[pltpu.SemaphoreType.DMA(()), pltpu.SemaphoreType.DMA(())],
    )
    return pl.pallas_call(
        functools.partial(_dispatch_kernel, td=td, n_tiles=lay.slots // TM_FFN),
        grid_spec=grid_spec,
        out_shape=jax.ShapeDtypeStruct((lay.slots, D_MODEL), F32),
        compiler_params=pltpu.CompilerParams(dimension_semantics=("arbitrary",)),
        name="dispatch",
    )(zstart, zcount, n_used, pos3, hn, zero)


def _combine_kernel(pos_ref, gate_ref, h_ref, gf_ref, y_ref, o_ref, buf_ref, sem, *, tc):
    def issue(r, c):
        for k in range(2):
            pltpu.make_async_copy(y_ref.at[pl.ds(pos_ref[k, r], 1)],
                                  buf_ref.at[k, pl.ds(r, 1)], sem).start()
        return c

    lax.fori_loop(0, tc, issue, 0)
    for k in range(2):
        pltpu.make_async_copy(y_ref.at[pl.ds(0, tc)], buf_ref.at[k], sem).wait()
    gate = gate_ref[...]
    y = gate[:, 0:1] * buf_ref[0] + gate[:, 1:2] * buf_ref[1]
    o_ref[...] = _rms(h_ref[...] + y, gf_ref[...]).astype(o_ref.dtype)


def _combine(h1, y_sorted, pos, gates, g_final, row0, batch, tokens):
    tc = TM_OUT
    per = tokens // tc
    el = pl.Element
    seq = N_META + tokens

    def grid_rows(a):
        a = a[:, row0:row0 + batch * seq].reshape(a.shape[0], batch, seq)[:, :, N_META:]
        return a.reshape(a.shape[0], batch * tokens)

    pos_g = grid_rows(pos).reshape(2, batch * per, tc).transpose(1, 0, 2)
    gates_g = grid_rows(gates).T

    def rmap(t):
        return (pl.multiple_of(row0 + N_META + (t // per) * seq + (t % per) * tc, ROW_MULT), 0)

    return pl.pallas_call(
        functools.partial(_combine_kernel, tc=tc),
        grid=(batch * per,),
        in_specs=[
            pl.BlockSpec((None, 2, tc), lambda t: (t, 0, 0), memory_space=pltpu.SMEM),
            pl.BlockSpec((tc, 2), lambda t: (t, 0)),
            pl.BlockSpec((el(tc), el(D_MODEL)), rmap),
            pl.BlockSpec((1, D_MODEL), lambda t: (0, 0)),
            pl.BlockSpec(memory_space=pl.ANY),
        ],
        out_specs=pl.BlockSpec((None, tc, D_MODEL), lambda t: (t // per, t % per, 0)),
        out_shape=jax.ShapeDtypeStruct((batch, tokens, D_MODEL), F32),
        scratch_shapes=[pltpu.VMEM((2, tc, D_MODEL), F32), pltpu.SemaphoreType.DMA(())],
        compiler_params=pltpu.CompilerParams(
            dimension_semantics=("arbitrary",), vmem_limit_bytes=VMEM_LIMIT),
        name="combine",
    )(pos_g, gates_g, h1, g_final.reshape(1, D_MODEL), y_sorted)


def _routing_tables(route, cnt, lay):
    tmx = TM_FFN
    counts = cnt[:, 0].astype(jnp.int32)
    padded = (counts + tmx - 1) // tmx * tmx
    ends = jnp.cumsum(padded)
    offs = ends - padded
    e_idx = route[0:2].astype(jnp.int32)
    rank = route[4:6].astype(jnp.int32)
    onehot = e_idx[:, :, None] == jnp.arange(N_EXPERTS, dtype=jnp.int32)
    pos = jnp.sum(jnp.where(onehot, offs, 0), axis=-1) + rank
    pos = jnp.where(jnp.arange(pos.shape[1])[None, :] < lay.n_valid, pos, 0)
    n_tiles = lay.slots // tmx
    tile_start = jnp.arange(n_tiles, dtype=jnp.int32) * tmx
    tile_expert = jnp.minimum(jnp.sum(tile_start[:, None] >= ends[None, :], axis=1),
                              N_EXPERTS - 1).astype(jnp.int32)
    n_used = (ends[-1] // tmx).reshape(1).astype(jnp.int32)
    return pos, route[2:4], tile_expert, n_used, offs + counts, padded - counts


def _forward(xs, meta_tokens, ln_mix, w_in, rpb, meta_bias, pool_w, pool_scale, g_attn_out,
             g_pool_out, w_out, ln_ffn, ffn_w_gate, ffn_w_up, ffn_w_down, router, moe_w_gate,
             moe_w_up, moe_w_down, g_final):
    groups = [(x.shape[0], x.shape[1]) for x in xs]
    lay = _Layout(groups)
    depth = ln_mix.shape[0]
    assert depth == 2

    parts = []
    for x in xs:
        for b in range(x.shape[0]):
            parts += [meta_tokens.astype(F32), x[b]]
    parts.append(jnp.zeros((lay.rows - lay.n_valid, D_MODEL), F32))
    h = jnp.concatenate(parts, axis=0)

    outs = None
    for layer in range(depth):
        qkv, u = _in_proj(h, ln_mix[layer], w_in[layer].astype(BF16))
        bias8, mbias = _attn_bias_tables(rpb[layer], meta_bias[layer])
        o_attn = jnp.zeros((lay.rows, ATTN_WIDTH), BF16)
        o_attn = _attention_grid(qkv, o_attn, bias8, mbias, lay)
        o_attn = _attention_meta(qkv, o_attn, meta_bias[layer].astype(F32), lay)
        mix_args = (o_attn, u, h, lay, pool_w[layer].astype(BF16), pool_scale[layer],
                    g_attn_out[layer], g_pool_out[layer], w_out[layer].astype(BF16), ln_ffn[layer])
        i = layer // 2
        if layer % 2 == 0:
            h1, hn = _mix_out(*mix_args)
            n_tiles = lay.rows // TM_FFN
            h = _ffn(hn, ffn_w_gate[i:i + 1].astype(BF16), ffn_w_up[i:i + 1].astype(BF16),
                     ffn_w_down[i:i + 1].astype(BF16), jnp.zeros((n_tiles,), jnp.int32),
                     jnp.full((1,), n_tiles, jnp.int32), res=h1)
        else:
            rt = router[i].T
            rt_hi = rt.astype(BF16)
            rt_lo = (rt - rt_hi.astype(F32)).astype(BF16)
            zpad = jnp.zeros_like(rt_hi)
            router_t = jnp.concatenate([rt_hi, zpad, rt_lo, zpad], axis=0)
            h1, hn, route, cnt = _mix_out(*mix_args, router_t=router_t)
            pos, gates, tile_expert, n_used, zstart, zcount = _routing_tables(route, cnt, lay)
            td = lay.td
            pos3 = pos[:, :lay.n_valid].reshape(2, lay.n_valid // td, td).transpose(1, 0, 2)
            x_sorted = _dispatch(hn, pos3, zstart, zcount, n_used, lay)
            y_sorted = _ffn(x_sorted, moe_w_gate[i].astype(BF16), moe_w_up[i].astype(BF16),
                            moe_w_down[i].astype(BF16), tile_expert, n_used)
            outs = []
            seq = 0
            for b, t in groups:
                outs.append(_combine(h1, y_sorted, pos, gates, g_final, lay.seq_start[seq], b, t))
                seq += b
    return tuple(outs)


def kernel(x_prompt, x_sample, meta_tokens, ln_mix, w_in, rpb, meta_bias, pool_w, pool_scale,
           g_attn_out, g_pool_out, w_out, ln_ffn, ffn_w_gate, ffn_w_up, ffn_w_down, router,
           moe_w_gate, moe_w_up, moe_w_down, g_final):
    return _forward((x_prompt, x_sample), meta_tokens, ln_mix, w_in, rpb, meta_bias, pool_w,
                    pool_scale, g_attn_out, g_pool_out, w_out, ln_ffn, ffn_w_gate, ffn_w_up,
                    ffn_w_down, router, moe_w_gate, moe_w_up, moe_w_down, g_final)
```

```python
import functools

import numpy as np
import jax
import jax.numpy as jnp
from jax import lax
from jax.experimental import pallas as pl
from jax.experimental.pallas import tpu as pltpu

F32 = jnp.float32
BF16 = jnp.bfloat16

D_MODEL = 2048
GRID_W = 64
N_META = 16
ATTN_HEADS = 16
HEAD_DIM = 64
ATTN_WIDTH = ATTN_HEADS * HEAD_DIM
POOL_WIDTH = D_MODEL - ATTN_WIDTH
POOL_WINDOWS = (2, 4, 8, 16)
POOL_GROUP = POOL_WIDTH // len(POOL_WINDOWS)
POOL_HALO = max(POOL_WINDOWS) // 2
NA_ROWS = 8
NA_COLS = 16
D_FF = 5632
N_EXPERTS = 8
EPS = 1e-6
NEG = -1e30

LANES = 128
HEAD_PAIR = LANES // HEAD_DIM
VMEM_LIMIT = 56 * 1024 * 1024

TM_IN = 1024
TN_IN = 1024
TM_MIX = 256
TM_FFN = 512
TF_FFN = 512
TM_OUT = 256
ATTN_CHUNK = 32
ATTN_BLOCK = 8
ROW_ALIGN = 1024
ROW_MULT = 16


def _round_up(x, m):
    return (x + m - 1) // m * m


def _rms(x, g):
    return x * lax.rsqrt(jnp.mean(x * x, axis=-1, keepdims=True) + EPS) * g


def _dot(a, b):
    return jnp.dot(a, b, preferred_element_type=F32)


def _dot_nt(a, b):
    return lax.dot_general(a, b, (((1,), (1,)), ((), ())), preferred_element_type=F32)


class _Layout:
    def __init__(self, groups):
        self.groups = groups
        self.seq_start, self.seq_tokens = [], []
        row = 0
        for b, t in groups:
            assert t % (GRID_W * NA_ROWS) == 0
            for _ in range(b):
                self.seq_start.append(row)
                self.seq_tokens.append(t)
                row += N_META + t
        self.n_valid = row
        self.rows = _round_up(row, ROW_ALIGN)
        pos = np.zeros((self.rows, 1), np.int32)
        rem = np.ones((self.rows, 1), np.int32)
        for s, t in zip(self.seq_start, self.seq_tokens):
            n = N_META + t
            pos[s:s + n, 0] = np.arange(n)
            rem[s:s + n, 0] = n - np.arange(n)
        self.pos, self.rem = pos, rem
        items = []
        self.chunk = ATTN_CHUNK
        self.kwin = self.chunk + NA_ROWS
        for s, t in zip(self.seq_start, self.seq_tokens):
            g = t // GRID_W
            assert g % self.chunk == 0
            for c in range(g // self.chunk):
                r_base = c * self.chunk
                kv_row0 = int(np.clip(r_base - NA_ROWS // 2, 0, max(g - self.kwin, 0)))
                items.append((s + N_META + r_base * GRID_W, s + N_META + kv_row0 * GRID_W,
                              s, r_base, g, kv_row0))
        self.items = np.asarray(items, np.int32)
        assert int(self.items[:, 1].max()) + self.kwin * GRID_W <= self.rows
        self.td = max(d for d in range(8, 513, 8) if self.n_valid % d == 0)
        self.slots = _round_up(2 * self.n_valid + N_EXPERTS * (TM_FFN - 1), TM_FFN)


def _in_proj_kernel(x_ref, g_ref, w_ref, qkv_ref, u_ref, xn_ref):
    j = pl.program_id(1)

    @pl.when(j == 0)
    def _():
        xn_ref[...] = _rms(x_ref[...], g_ref[...]).astype(BF16)

    z = _dot(xn_ref[...], w_ref[...])

    @pl.when(j == 0)
    def _():
        qkv_ref[...] = (z * (HEAD_DIM ** -0.5)).astype(BF16)

    @pl.when(jnp.logical_and(j > 0, j < 3))
    def _():
        qkv_ref[...] = z.astype(BF16)

    @pl.when(j == 3)
    def _():
        u_ref[...] = z


def _in_proj(h, g, w_bf):
    rows = h.shape[0]
    assert TN_IN == ATTN_WIDTH == POOL_WIDTH
    return pl.pallas_call(
        _in_proj_kernel,
        grid=(rows // TM_IN, 4),
        in_specs=[
            pl.BlockSpec((TM_IN, D_MODEL), lambda i, j: (i, 0)),
            pl.BlockSpec((1, D_MODEL), lambda i, j: (0, 0)),
            pl.BlockSpec((D_MODEL, TN_IN), lambda i, j: (0, j)),
        ],
        out_specs=[
            pl.BlockSpec((TM_IN, TN_IN), lambda i, j: (i, jnp.minimum(j, 2))),
            pl.BlockSpec((TM_IN, TN_IN), lambda i, j: (i, 0)),
        ],
        out_shape=[
            jax.ShapeDtypeStruct((rows, 3 * ATTN_WIDTH), BF16),
            jax.ShapeDtypeStruct((rows, POOL_WIDTH), F32),
        ],
        scratch_shapes=[pltpu.VMEM((TM_IN, D_MODEL), BF16)],
        compiler_params=pltpu.CompilerParams(
            dimension_semantics=("arbitrary", "arbitrary"), vmem_limit_bytes=VMEM_LIMIT),
        name="in_proj",
    )(h, g.reshape(1, D_MODEL), w_bf)


def _attn_kernel(qoff_ref, kvoff_ref, moff_ref, rbase_ref, gsz_ref, kvrow_ref,
                 q_ref, k_ref, v_ref, km_ref, vm_ref, bias_ref, mb_ref, o_ref,
                 ka_ref, kb_ref, kme_ref, vme_ref, s_ref, p_ref, l_ref, m_ref, *, chunk):
    it = pl.program_id(1)
    r_base = rbase_ref[it]
    g_rows = gsz_ref[it]
    kv_row0 = kvrow_ref[it]
    is_a = lax.broadcasted_iota(jnp.int32, (1, LANES), 1) < HEAD_DIM

    k = k_ref[...]
    kz = jnp.zeros_like(k)
    ka_ref[...] = jnp.where(is_a, k, kz)
    kb_ref[...] = jnp.where(is_a, kz, k)
    km = km_ref[...]
    kmz = jnp.zeros_like(km)
    kme_ref[...] = jnp.zeros_like(kme_ref)
    kme_ref[0:N_META, :] = jnp.where(is_a, km, kmz)
    kme_ref[N_META:2 * N_META, :] = jnp.where(is_a, kmz, km)
    vme_ref[...] = jnp.zeros_like(vme_ref)
    vme_ref[0:N_META, :] = vm_ref[...]
    vme_ref[N_META:2 * N_META, :] = vm_ref[...]

    nk = NA_ROWS * GRID_W
    krefs = (ka_ref, kb_ref)

    def row_window(r):
        rg = r_base + r
        rs = jnp.clip(rg - NA_ROWS // 2, 0, g_rows - NA_ROWS)
        var = rs - rg + (NA_ROWS - 1)
        kl = pl.multiple_of((rs - kv_row0) * GRID_W, GRID_W)
        q0 = pl.multiple_of(r * GRID_W, GRID_W)
        return var, kl, q0

    def block(bi, carry):
        for rr in range(ATTN_BLOCK):
            var, kl, q0 = row_window(bi * ATTN_BLOCK + rr)
            rows = slice(rr * GRID_W, (rr + 1) * GRID_W)
            q = q_ref[pl.ds(q0, GRID_W), :]
            s_meta = _dot_nt(q, kme_ref[...])
            for hd in range(HEAD_PAIR):
                sc = _dot_nt(q, krefs[hd][pl.ds(kl, nk), :]) + bias_ref[hd, var]
                sm = s_meta + mb_ref[hd]
                s_ref[hd, rows, 0:nk] = sc
                s_ref[hd, rows, nk:nk + LANES] = sm
                for c in range(nk // LANES):
                    sm = jnp.maximum(sm, sc[:, c * LANES:(c + 1) * LANES])
                m_ref[hd, rows, :] = sm
        for hd in range(HEAD_PAIR):
            for rr in range(ATTN_BLOCK):
                rows = slice(rr * GRID_W, (rr + 1) * GRID_W)
                m = jnp.max(m_ref[hd, rows, :], axis=1, keepdims=True)
                acc = None
                for c in range(nk // LANES + 1):
                    cols = slice(c * LANES, (c + 1) * LANES)
                    p = jnp.exp(s_ref[hd, rows, cols] - m)
                    acc = p if acc is None else acc + p
                    p_ref[hd, rows, cols] = p.astype(BF16)
                inv = 1.0 / jnp.sum(acc, axis=1, keepdims=True)
                l_ref[hd, rows, :] = jnp.broadcast_to(inv, (GRID_W, LANES))
        for rr in range(ATTN_BLOCK):
            var, kl, q0 = row_window(bi * ATTN_BLOCK + rr)
            rows = slice(rr * GRID_W, (rr + 1) * GRID_W)
            vv = v_ref[pl.ds(kl, nk), :]
            outs = []
            for hd in range(HEAD_PAIR):
                o = (_dot(p_ref[hd, rows, 0:nk], vv)
                     + _dot(p_ref[hd, rows, nk:nk + LANES], vme_ref[...]))
                outs.append(o * l_ref[hd, rows, :])
            o_ref[pl.ds(q0, GRID_W), :] = jnp.where(is_a, outs[0], outs[1]).astype(o_ref.dtype)
        return carry

    lax.fori_loop(0, chunk // ATTN_BLOCK, block, 0)


def _attention_grid(qkv, o_init, bias8, mbias, lay):
    rows = qkv.shape[0]
    qrows = lay.chunk * GRID_W
    krows = lay.kwin * GRID_W
    n_items = lay.items.shape[0]
    n_pairs = ATTN_HEADS // HEAD_PAIR
    tabs = [jnp.asarray(lay.items[:, c]) for c in range(6)]
    el = pl.Element

    def qmap(hp, it, qo, ko, mo, rb, gs, kr):
        return (pl.multiple_of(qo[it], ROW_MULT), pl.multiple_of(hp * LANES, LANES))

    def kmap(col0):
        def f(hp, it, qo, ko, mo, rb, gs, kr):
            return (pl.multiple_of(ko[it], ROW_MULT), pl.multiple_of(col0 + hp * LANES, LANES))
        return f

    def mmap(col0):
        def f(hp, it, qo, ko, mo, rb, gs, kr):
            return (pl.multiple_of(mo[it], ROW_MULT), pl.multiple_of(col0 + hp * LANES, LANES))
        return f

    grid_spec = pltpu.PrefetchScalarGridSpec(
        num_scalar_prefetch=6,
        grid=(n_pairs, n_items),
        in_specs=[
            pl.BlockSpec((el(qrows), el(LANES)), qmap),
            pl.BlockSpec((el(krows), el(LANES)), kmap(ATTN_WIDTH)),
            pl.BlockSpec((el(krows), el(LANES)), kmap(2 * ATTN_WIDTH)),
            pl.BlockSpec((el(N_META), el(LANES)), mmap(ATTN_WIDTH)),
            pl.BlockSpec((el(N_META), el(LANES)), mmap(2 * ATTN_WIDTH)),
            pl.BlockSpec((HEAD_PAIR, NA_ROWS, GRID_W, NA_ROWS * GRID_W),
                         lambda hp, it, *_: (hp, 0, 0, 0)),
            pl.BlockSpec((HEAD_PAIR, 1, LANES), lambda hp, it, *_: (hp, 0, 0)),
            pl.BlockSpec(memory_space=pl.ANY),
        ],
        out_specs=pl.BlockSpec((el(qrows), el(LANES)), qmap),
        scratch_shapes=[
            pltpu.VMEM((krows, LANES), BF16),
            pltpu.VMEM((krows, LANES), BF16),
            pltpu.VMEM((LANES, LANES), BF16),
            pltpu.VMEM((LANES, LANES), BF16),
            pltpu.VMEM((HEAD_PAIR, ATTN_BLOCK * GRID_W, NA_ROWS * GRID_W + LANES), F32),
            pltpu.VMEM((HEAD_PAIR, ATTN_BLOCK * GRID_W, NA_ROWS * GRID_W + LANES), BF16),
            pltpu.VMEM((HEAD_PAIR, ATTN_BLOCK * GRID_W, LANES), F32),
            pltpu.VMEM((HEAD_PAIR, ATTN_BLOCK * GRID_W, LANES), F32),
        ],
    )

    def kern(qo, ko, mo, rb, gs, kr, q, k, v, km, vm, b, mb, o_in, o, *scr):
        del o_in
        _attn_kernel(qo, ko, mo, rb, gs, kr, q, k, v, km, vm, b, mb, o, *scr, chunk=lay.chunk)

    return pl.pallas_call(
        kern,
        grid_spec=grid_spec,
        out_shape=jax.ShapeDtypeStruct((rows, ATTN_WIDTH), BF16),
        input_output_aliases={13: 0},
        compiler_params=pltpu.CompilerParams(
            dimension_semantics=("arbitrary", "arbitrary"), vmem_limit_bytes=VMEM_LIMIT),
        name="attn_grid",
    )(*tabs, qkv, qkv, qkv, qkv, qkv, bias8, mbias, o_init)


def _attn_meta_kernel(moff_ref, x_ref, mb_ref, o_in_ref, o_ref):
    del moff_ref, o_in_ref
    x = x_ref[...].astype(F32)
    outs = []
    for h in range(ATTN_HEADS):
        q = x[:, h * HEAD_DIM:(h + 1) * HEAD_DIM].astype(BF16)
        k = x[:, ATTN_WIDTH + h * HEAD_DIM:ATTN_WIDTH + (h + 1) * HEAD_DIM].astype(BF16)
        v = x[:, 2 * ATTN_WIDTH + h * HEAD_DIM:2 * ATTN_WIDTH + (h + 1) * HEAD_DIM].astype(BF16)
        s = _dot_nt(q, k) + mb_ref[h:h + 1, :]
        m = jnp.max(s, axis=1, keepdims=True)
        p = jnp.exp(s - m)
        l = jnp.sum(p, axis=1, keepdims=True)
        outs.append(_dot(p.astype(BF16), v) / l)
    o_ref[...] = jnp.concatenate(outs, axis=1).astype(o_ref.dtype)


def _attention_meta(qkv, o_init, meta_bias, lay):
    rows = qkv.shape[0]
    n_seq = len(lay.seq_start)
    moff = jnp.asarray(np.asarray(lay.seq_start, np.int32))
    el = pl.Element
    grid_spec = pltpu.PrefetchScalarGridSpec(
        num_scalar_prefetch=1,
        grid=(n_seq,),
        in_specs=[
            pl.BlockSpec((el(N_META), el(3 * ATTN_WIDTH)),
                         lambda b, mo: (pl.multiple_of(mo[b], ROW_MULT), 0)),
            pl.BlockSpec((ATTN_HEADS, N_META), lambda b, mo: (0, 0)),
            pl.BlockSpec(memory_space=pl.ANY),
        ],
        out_specs=pl.BlockSpec((el(N_META), el(ATTN_WIDTH)),
                               lambda b, mo: (pl.multiple_of(mo[b], ROW_MULT), 0)),
    )
    return pl.pallas_call(
        _attn_meta_kernel,
        grid_spec=grid_spec,
        out_shape=jax.ShapeDtypeStruct((rows, ATTN_WIDTH), BF16),
        input_output_aliases={3: 0},
        compiler_params=pltpu.CompilerParams(dimension_semantics=("arbitrary",)),
        name="attn_meta",
    )(moff, qkv, meta_bias, o_init)


def _attn_bias_tables(rpb, meta_bias):
    cq = np.arange(GRID_W)
    cs = np.clip(cq - NA_COLS // 2, 0, GRID_W - NA_COLS)
    ck = np.arange(GRID_W)
    valid = (ck[None, :] >= cs[:, None]) & (ck[None, :] < cs[:, None] + NA_COLS)
    dc = ck[None, :] - cq[:, None] + NA_COLS - 1
    onehot = (np.arange(2 * NA_COLS - 1)[:, None, None] == dc[None]) & valid[None]
    t = jnp.einsum('hrd,dqk->hqrk', rpb.astype(F32), jnp.asarray(onehot, F32),
                   precision=lax.Precision.HIGHEST)
    t = t + jnp.asarray(np.where(valid, 0.0, NEG), F32)[None, :, None, :]
    bias8 = jnp.stack([t[:, :, v:v + NA_ROWS, :].reshape(ATTN_HEADS, GRID_W, NA_ROWS * GRID_W)
                       for v in range(NA_ROWS)], axis=1)
    lane = np.arange(LANES)
    head = np.arange(ATTN_HEADS)
    slot = (head % HEAD_PAIR) * N_META
    src = np.clip(lane[None, :] - slot[:, None], 0, N_META - 1)
    ok = (lane[None, :] >= slot[:, None]) & (lane[None, :] < slot[:, None] + N_META)
    mb = jnp.where(jnp.asarray(ok), jnp.take_along_axis(meta_bias.astype(F32), jnp.asarray(src), 1), NEG)
    return bias8, mb.reshape(ATTN_HEADS, 1, LANES)


def _mix_out_kernel(*refs, moe, n_valid):
    if moe:
        (oa_ref, u_ref, up_ref, un_ref, pos_ref, rem_ref, pw_ref, ps_ref, ga_ref, gp_ref, wo_ref,
         h_ref, lf_ref, rt_ref, tri_ref, h1_ref, hn_ref, route_ref, cnt_ref, ext_ref, run_ref) = refs
    else:
        (oa_ref, u_ref, up_ref, un_ref, pos_ref, rem_ref, pw_ref, ps_ref, ga_ref, gp_ref, wo_ref,
         h_ref, lf_ref, h1_ref, hn_ref, ext_ref) = refs
    tm = u_ref.shape[0]
    hal = POOL_HALO
    ext_ref[0:hal, :] = up_ref[...]
    ext_ref[hal:hal + tm, :] = u_ref[...]
    ext_ref[hal + tm:hal + tm + hal, :] = un_ref[...]
    pos = pos_ref[...]
    rem = rem_ref[...]

    mixed = []
    for g, w in enumerate(POOL_WINDOWS):
        half = w // 2
        c0 = g * POOL_GROUP
        acc = ext_ref[hal:hal + tm, c0:c0 + POOL_GROUP]
        centre = acc
        for kk in range(-half, half):
            if kk == 0:
                continue
            sl = ext_ref[hal + kk:hal + kk + tm, c0:c0 + POOL_GROUP]
            ok = (pos >= -kk) if kk < 0 else (rem > kk)
            acc = acc + jnp.where(ok, sl, 0.0)
        cnt = (jnp.minimum(pos, half) + jnp.minimum(rem, half)).astype(F32)
        pooled = (acc / cnt - centre).astype(BF16)
        mixed.append(_dot(pooled, pw_ref[g]))
    o_pool = jnp.concatenate(mixed, axis=1) * ps_ref[...]
    n_pool = _rms(o_pool, gp_ref[...]).astype(BF16)
    n_attn = _rms(oa_ref[...].astype(F32), ga_ref[...]).astype(BF16)
    mix = _dot(n_attn, wo_ref[0:ATTN_WIDTH, :]) + _dot(n_pool, wo_ref[ATTN_WIDTH:D_MODEL, :])
    h1 = h_ref[...] + mix
    h1_ref[...] = h1
    hn = _rms(h1, lf_ref[...])
    if not moe:
        hn_ref[...] = hn.astype(BF16)
        return

    hn_ref[...] = hn
    i = pl.program_id(0)

    @pl.when(i == 0)
    def _():
        run_ref[...] = jnp.zeros_like(run_ref)

    hi = hn.astype(BF16)
    lo = (hn - hi.astype(F32)).astype(BF16)
    l1 = _dot_nt(rt_ref[...], hi)
    l2 = _dot_nt(rt_ref[0:16, :], lo)
    lg = l1[0:N_EXPERTS] + l1[16:16 + N_EXPERTS] + l2[0:N_EXPERTS]
    eidx = lax.broadcasted_iota(jnp.int32, (N_EXPERTS, tm), 0).astype(F32)
    none = float(N_EXPERTS)
    m1 = jnp.max(lg, axis=0, keepdims=True)
    i1 = jnp.min(jnp.where(lg == m1, eidx, none), axis=0, keepdims=True)
    sel1 = eidx == i1
    lg2 = jnp.where(sel1, -jnp.inf, lg)
    m2 = jnp.max(lg2, axis=0, keepdims=True)
    i2 = jnp.min(jnp.where(lg2 == m2, eidx, none), axis=0, keepdims=True)
    sel2 = eidx == i2
    t = jnp.exp(m2 - m1)
    g1 = 1.0 / (1.0 + t)
    g2 = t / (1.0 + t)
    rowid = i * tm + lax.broadcasted_iota(jnp.int32, (1, tm), 1)
    valid = rowid < n_valid
    c = jnp.where(jnp.logical_and(jnp.logical_or(sel1, sel2), valid), 1.0, 0.0)
    c16 = jnp.concatenate([c, jnp.zeros_like(c)], axis=0).astype(BF16)
    cs = _dot(c16, tri_ref[...])[0:N_EXPERTS]
    rank_all = run_ref[:, 0:1] + cs - 1.0
    rank1 = jnp.sum(jnp.where(sel1, rank_all, 0.0), axis=0, keepdims=True)
    rank2 = jnp.sum(jnp.where(sel2, rank_all, 0.0), axis=0, keepdims=True)
    run_new = run_ref[...] + jnp.sum(c, axis=1, keepdims=True)
    run_ref[...] = run_new
    cnt_ref[...] = run_new
    route = jnp.zeros((N_EXPERTS, tm), F32)
    for k, row in enumerate((i1, i2, g1, g2, rank1, rank2)):
        route = jnp.where(eidx == float(k), row, route)
    route_ref[...] = route


def _mix_out(o_attn, u, h, lay, pool_w_bf, pool_scale, g_attn, g_pool, w_out_bf, ln_ffn,
             router_t=None):
    rows = h.shape[0]
    tm = TM_MIX
    moe = router_t is not None
    blk8 = tm // POOL_HALO
    n8 = rows // POOL_HALO
    row1 = lambda a: a.reshape(1, -1)
    in_specs = [
        pl.BlockSpec((tm, ATTN_WIDTH), lambda i: (i, 0)),
        pl.BlockSpec((tm, POOL_WIDTH), lambda i: (i, 0)),
        pl.BlockSpec((POOL_HALO, POOL_WIDTH), lambda i: (jnp.maximum(i * blk8 - 1, 0), 0)),
        pl.BlockSpec((POOL_HALO, POOL_WIDTH), lambda i: (jnp.minimum((i + 1) * blk8, n8 - 1), 0)),
        pl.BlockSpec((tm, 1), lambda i: (i, 0)),
        pl.BlockSpec((tm, 1), lambda i: (i, 0)),
        pl.BlockSpec((len(POOL_WINDOWS), POOL_GROUP, POOL_GROUP), lambda i: (0, 0, 0)),
        pl.BlockSpec((1, POOL_WIDTH), lambda i: (0, 0)),
        pl.BlockSpec((1, ATTN_WIDTH), lambda i: (0, 0)),
        pl.BlockSpec((1, POOL_WIDTH), lambda i: (0, 0)),
        pl.BlockSpec((D_MODEL, D_MODEL), lambda i: (0, 0), pipeline_mode=pl.Buffered(1)),
        pl.BlockSpec((tm, D_MODEL), lambda i: (i, 0)),
        pl.BlockSpec((1, D_MODEL), lambda i: (0, 0)),
    ]
    args = [o_attn, u, u, u, jnp.asarray(lay.pos), jnp.asarray(lay.rem), pool_w_bf,
            row1(pool_scale), row1(g_attn), row1(g_pool), w_out_bf, h, row1(ln_ffn)]
    out_specs = [pl.BlockSpec((tm, D_MODEL), lambda i: (i, 0)),
                 pl.BlockSpec((tm, D_MODEL), lambda i: (i, 0))]
    out_shape = [jax.ShapeDtypeStruct((rows, D_MODEL), F32),
                 jax.ShapeDtypeStruct((rows, D_MODEL), F32 if moe else BF16)]
    scratch = [pltpu.VMEM((tm + 2 * POOL_HALO, POOL_WIDTH), F32)]
    if moe:
        tri = jnp.asarray(np.triu(np.ones((tm, tm), np.float32)), BF16)
        in_specs += [pl.BlockSpec((32, D_MODEL), lambda i: (0, 0)),
                     pl.BlockSpec((tm, tm), lambda i: (0, 0))]
        args += [router_t, tri]
        out_specs += [pl.BlockSpec((N_EXPERTS, tm), lambda i: (0, i)),
                      pl.BlockSpec((N_EXPERTS, LANES), lambda i: (0, 0))]
        out_shape += [jax.ShapeDtypeStruct((N_EXPERTS, rows), F32),
                      jax.ShapeDtypeStruct((N_EXPERTS, LANES), F32)]
        scratch += [pltpu.VMEM((N_EXPERTS, LANES), F32)]
    return pl.pallas_call(
        functools.partial(_mix_out_kernel, moe=moe, n_valid=lay.n_valid),
        grid=(rows // tm,),
        in_specs=in_specs,
        out_specs=out_specs,
        out_shape=out_shape,
        scratch_shapes=scratch,
        compiler_params=pltpu.CompilerParams(
            dimension_semantics=("arbitrary",), vmem_limit_bytes=VMEM_LIMIT),
        name="mix_out_moe" if moe else "mix_out",
    )(*args)


def _swiglu_step(x, wg_ref, wu_ref, wd_ref):
    gate = _dot(x, wg_ref[...])
    up = _dot(x, wu_ref[...])
    mid = (gate / (1.0 + jnp.exp(-gate))) * up
    return _dot(mid.astype(BF16), wd_ref[...])


def _ffn_dense_kernel(x_ref, wg_ref, wu_ref, wd_ref, res_ref, o_ref):
    j = pl.program_id(1)
    y = _swiglu_step(x_ref[...], wg_ref, wu_ref, wd_ref)

    @pl.when(j == 0)
    def _():
        o_ref[...] = res_ref[...] + y

    @pl.when(j > 0)
    def _():
        o_ref[...] += y


def _ffn_dense(x, wg_bf, wu_bf, wd_bf, res):
    rows = x.shape[0]
    tm, tf = TM_FFN, TF_FFN
    return pl.pallas_call(
        _ffn_dense_kernel,
        grid=(rows // tm, D_FF // tf),
        in_specs=[
            pl.BlockSpec((tm, D_MODEL), lambda i, j: (i, 0)),
            pl.BlockSpec((D_MODEL, tf), lambda i, j: (0, j)),
            pl.BlockSpec((D_MODEL, tf), lambda i, j: (0, j)),
            pl.BlockSpec((tf, D_MODEL), lambda i, j: (j, 0)),
            pl.BlockSpec((tm, D_MODEL), lambda i, j: (i, 0)),
        ],
        out_specs=pl.BlockSpec((tm, D_MODEL), lambda i, j: (i, 0)),
        out_shape=jax.ShapeDtypeStruct((rows, D_MODEL), F32),
        compiler_params=pltpu.CompilerParams(
            dimension_semantics=("arbitrary", "arbitrary"), vmem_limit_bytes=VMEM_LIMIT),
        name="ffn_dense",
    )(x, wg_bf, wu_bf, wd_bf, res)


def _ffn_moe_kernel(te_ref, nu_ref, src_ref, hn_ref, wg_ref, wu_ref, wd_ref, o_ref,
                    xbuf_ref, xb_ref, sem):
    del te_ref
    tm = o_ref.shape[0]
    i = pl.program_id(0)
    j = pl.program_id(1)
    n_used = nu_ref[0]
    used = i < n_used

    def tile_rows(slot):
        return pltpu.make_async_copy(hn_ref.at[pl.ds(0, tm)], xbuf_ref.at[slot], sem.at[slot])

    def start_gather(tile, slot):
        base = tile * tm

        def body(r, c):
            pltpu.make_async_copy(hn_ref.at[pl.ds(src_ref[base + r], 1)],
                                  xbuf_ref.at[slot, pl.ds(r, 1)], sem.at[slot]).start()
            return c

        lax.fori_loop(0, tm, body, 0, unroll=8)

    @pl.when(jnp.logical_and(jnp.logical_not(used), j == 0))
    def _():
        o_ref[...] = jnp.zeros_like(o_ref)

    @pl.when(jnp.logical_and(used, j == 0))
    def _():
        slot = lax.rem(i, 2)

        @pl.when(i == 0)
        def _():
            start_gather(0, 0)

        tile_rows(slot).wait()
        xb_ref[...] = xbuf_ref[slot].astype(BF16)

        @pl.when(i + 1 < n_used)
        def _():
            start_gather(i + 1, 1 - slot)

    @pl.when(used)
    def _():
        y = _swiglu_step(xb_ref[...], wg_ref, wu_ref, wd_ref)

        @pl.when(j == 0)
        def _():
            o_ref[...] = y

        @pl.when(j > 0)
        def _():
            o_ref[...] += y


def _ffn_moe(hn, src, wg_bf, wu_bf, wd_bf, tile_expert, n_used, lay):
    tm, tf = TM_FFN, TF_FFN
    nf = D_FF // tf

    def jj(i, j, nu):
        return jnp.where(i < nu[0], j, nf - 1)

    grid_spec = pltpu.PrefetchScalarGridSpec(
        num_scalar_prefetch=3,
        grid=(lay.slots // tm, nf),
        in_specs=[
            pl.BlockSpec(memory_space=pl.ANY),
            pl.BlockSpec((None, D_MODEL, tf), lambda i, j, te, nu, sr: (te[i], 0, jj(i, j, nu))),
            pl.BlockSpec((None, D_MODEL, tf), lambda i, j, te, nu, sr: (te[i], 0, jj(i, j, nu))),
            pl.BlockSpec((None, tf, D_MODEL), lambda i, j, te, nu, sr: (te[i], jj(i, j, nu), 0)),
        ],
        out_specs=pl.BlockSpec((tm, D_MODEL), lambda i, j, te, nu, sr: (i, 0)),
        scratch_shapes=[
            pltpu.VMEM((2, tm, D_MODEL), F32),
            pltpu.VMEM((tm, D_MODEL), BF16),
            pltpu.SemaphoreType.DMA((2,)),
        ],
    )
    return pl.pallas_call(
        _ffn_moe_kernel,
        grid_spec=grid_spec,
        out_shape=jax.ShapeDtypeStruct((lay.slots, D_MODEL), F32),
        compiler_params=pltpu.CompilerParams(
            dimension_semantics=("arbitrary", "arbitrary"), vmem_limit_bytes=VMEM_LIMIT),
        name="ffn_moe",
    )(tile_expert, n_used, src, hn, wg_bf, wu_bf, wd_bf)


def _slot_src_kernel(pos_ref, src_ref, *, td, slots):
    i = pl.program_id(0)

    @pl.when(i == 0)
    def _():
        def init(s, c):
            src_ref[s] = 0
            return c

        lax.fori_loop(0, slots, init, 0, unroll=8)

    base = i * td

    def body(r, c):
        src_ref[pos_ref[0, r]] = base + r
        src_ref[pos_ref[1, r]] = base + r
        return c

    lax.fori_loop(0, td, body, 0, unroll=8)


def _slot_sources(pos3, lay):
    td = lay.td
    return pl.pallas_call(
        functools.partial(_slot_src_kernel, td=td, slots=lay.slots),
        grid=(lay.n_valid // td,),
        in_specs=[pl.BlockSpec((None, 2, td), lambda i: (i, 0, 0), memory_space=pltpu.SMEM)],
        out_specs=pl.BlockSpec(memory_space=pltpu.SMEM),
        out_shape=jax.ShapeDtypeStruct((lay.slots,), jnp.int32),
        compiler_params=pltpu.CompilerParams(dimension_semantics=("arbitrary",)),
        name="slot_sources",
    )(pos3)


def _combine_kernel(pos_ref, gate_ref, h_ref, gf_ref, y_ref, o_ref, buf_ref, sem, *, tc):
    def issue(r, c):
        for k in range(2):
            pltpu.make_async_copy(y_ref.at[pl.ds(pos_ref[k, r], 1)],
                                  buf_ref.at[k, pl.ds(r, 1)], sem).start()
        return c

    lax.fori_loop(0, tc, issue, 0)
    for k in range(2):
        pltpu.make_async_copy(y_ref.at[pl.ds(0, tc)], buf_ref.at[k], sem).wait()
    gate = gate_ref[...]
    y = gate[:, 0:1] * buf_ref[0] + gate[:, 1:2] * buf_ref[1]
    o_ref[...] = _rms(h_ref[...] + y, gf_ref[...]).astype(o_ref.dtype)


def _combine(h1, y_sorted, pos, gates, g_final, row0, batch, tokens):
    tc = TM_OUT
    per = tokens // tc
    el = pl.Element
    seq = N_META + tokens

    def grid_rows(a):
        a = a[:, row0:row0 + batch * seq].reshape(a.shape[0], batch, seq)[:, :, N_META:]
        return a.reshape(a.shape[0], batch * tokens)

    pos_g = grid_rows(pos).reshape(2, batch * per, tc).transpose(1, 0, 2)
    gates_g = grid_rows(gates).T

    def rmap(t):
        return (pl.multiple_of(row0 + N_META + (t // per) * seq + (t % per) * tc, ROW_MULT), 0)

    return pl.pallas_call(
        functools.partial(_combine_kernel, tc=tc),
        grid=(batch * per,),
        in_specs=[
            pl.BlockSpec((None, 2, tc), lambda t: (t, 0, 0), memory_space=pltpu.SMEM),
            pl.BlockSpec((tc, 2), lambda t: (t, 0)),
            pl.BlockSpec((el(tc), el(D_MODEL)), rmap),
            pl.BlockSpec((1, D_MODEL), lambda t: (0, 0)),
            pl.BlockSpec(memory_space=pl.ANY),
        ],
        out_specs=pl.BlockSpec((None, tc, D_MODEL), lambda t: (t // per, t % per, 0)),
        out_shape=jax.ShapeDtypeStruct((batch, tokens, D_MODEL), F32),
        scratch_shapes=[pltpu.VMEM((2, tc, D_MODEL), F32), pltpu.SemaphoreType.DMA(())],
        compiler_params=pltpu.CompilerParams(
            dimension_semantics=("arbitrary",), vmem_limit_bytes=VMEM_LIMIT),
        name="combine",
    )(pos_g, gates_g, h1, g_final.reshape(1, D_MODEL), y_sorted)


def _routing_tables(route, cnt, lay):
    tmx = TM_FFN
    counts = cnt[:, 0].astype(jnp.int32)
    padded = (counts + tmx - 1) // tmx * tmx
    ends = jnp.cumsum(padded)
    offs = ends - padded
    e_idx = route[0:2].astype(jnp.int32)
    rank = route[4:6].astype(jnp.int32)
    onehot = e_idx[:, :, None] == jnp.arange(N_EXPERTS, dtype=jnp.int32)
    pos = jnp.sum(jnp.where(onehot, offs, 0), axis=-1) + rank
    pos = jnp.where(jnp.arange(pos.shape[1])[None, :] < lay.n_valid, pos, 0)
    n_tiles = lay.slots // tmx
    tile_start = jnp.arange(n_tiles, dtype=jnp.int32) * tmx
    tile_expert = jnp.minimum(jnp.sum(tile_start[:, None] >= ends[None, :], axis=1),
                              N_EXPERTS - 1).astype(jnp.int32)
    n_used = (ends[-1] // tmx).reshape(1).astype(jnp.int32)
    return pos, route[2:4], tile_expert, n_used


def _forward(xs, meta_tokens, ln_mix, w_in, rpb, meta_bias, pool_w, pool_scale, g_attn_out,
             g_pool_out, w_out, ln_ffn, ffn_w_gate, ffn_w_up, ffn_w_down, router, moe_w_gate,
             moe_w_up, moe_w_down, g_final):
    groups = [(x.shape[0], x.shape[1]) for x in xs]
    lay = _Layout(groups)
    depth = ln_mix.shape[0]
    assert depth == 2

    parts = []
    for x in xs:
        for b in range(x.shape[0]):
            parts += [meta_tokens.astype(F32), x[b]]
    parts.append(jnp.zeros((lay.rows - lay.n_valid, D_MODEL), F32))
    h = jnp.concatenate(parts, axis=0)

    outs = None
    for layer in range(depth):
        qkv, u = _in_proj(h, ln_mix[layer], w_in[layer].astype(BF16))
        bias8, mbias = _attn_bias_tables(rpb[layer], meta_bias[layer])
        o_attn = jnp.zeros((lay.rows, ATTN_WIDTH), BF16)
        o_attn = _attention_grid(qkv, o_attn, bias8, mbias, lay)
        o_attn = _attention_meta(qkv, o_attn, meta_bias[layer].astype(F32), lay)
        mix_args = (o_attn, u, h, lay, pool_w[layer].astype(BF16), pool_scale[layer],
                    g_attn_out[layer], g_pool_out[layer], w_out[layer].astype(BF16), ln_ffn[layer])
        i = layer // 2
        if layer % 2 == 0:
            h1, hn = _mix_out(*mix_args)
            h = _ffn_dense(hn, ffn_w_gate[i].astype(BF16), ffn_w_up[i].astype(BF16),
                           ffn_w_down[i].astype(BF16), h1)
        else:
            rt = router[i].T
            rt_hi = rt.astype(BF16)
            rt_lo = (rt - rt_hi.astype(F32)).astype(BF16)
            zpad = jnp.zeros_like(rt_hi)
            router_t = jnp.concatenate([rt_hi, zpad, rt_lo, zpad], axis=0)
            h1, hn, route, cnt = _mix_out(*mix_args, router_t=router_t)
            pos, gates, tile_expert, n_used = _routing_tables(route, cnt, lay)
            td = lay.td
            pos3 = pos[:, :lay.n_valid].reshape(2, lay.n_valid // td, td).transpose(1, 0, 2)
            src = _slot_sources(pos3, lay)
            y_sorted = _ffn_moe(hn, src, moe_w_gate[i].astype(BF16), moe_w_up[i].astype(BF16),
                                moe_w_down[i].astype(BF16), tile_expert, n_used, lay)
            outs = []
            seq = 0
            for b, t in groups:
                outs.append(_combine(h1, y_sorted, pos, gates, g_final, lay.seq_start[seq], b, t))
                seq += b
    return tuple(outs)


def kernel(x_prompt, x_sample, meta_tokens, ln_mix, w_in, rpb, meta_bias, pool_w, pool_scale,
           g_attn_out, g_pool_out, w_out, ln_ffn, ffn_w_gate, ffn_w_up, ffn_w_down, router,
           moe_w_gate, moe_w_up, moe_w_down, g_final):
    return _forward((x_prompt, x_sample), meta_tokens, ln_mix, w_in, rpb, meta_bias, pool_w,
                    pool_scale, g_attn_out, g_pool_out, w_out, ln_ffn, ffn_w_gate, ffn_w_up,
                    ffn_w_down, router, moe_w_gate, moe_w_up, moe_w_down, g_final)
```

```python
import functools

import numpy as np
import jax
import jax.numpy as jnp
from jax import lax
from jax.experimental import pallas as pl
from jax.experimental.pallas import tpu as pltpu

F32 = jnp.float32
BF16 = jnp.bfloat16

D_MODEL = 2048
GRID_W = 64
N_META = 16
ATTN_HEADS = 16
HEAD_DIM = 64
ATTN_WIDTH = ATTN_HEADS * HEAD_DIM
POOL_WIDTH = D_MODEL - ATTN_WIDTH
POOL_WINDOWS = (2, 4, 8, 16)
POOL_GROUP = POOL_WIDTH // len(POOL_WINDOWS)
POOL_HALO = max(POOL_WINDOWS) // 2
NA_ROWS = 8
NA_COLS = 16
D_FF = 5632
N_EXPERTS = 8
EPS = 1e-6
NEG = -1e30

LANES = 128
HEAD_PAIR = LANES // HEAD_DIM
VMEM_LIMIT = 56 * 1024 * 1024

TM_IN = 1024
TN_IN = 1024
TM_MIX = 256
TM_FFN = 512
TF_FFN = 512
D_FF_STEPS = D_FF // TF_FFN
GATHER_SHARE = (-(-TM_FFN // D_FF_STEPS) + 7) // 8 * 8
GATHER_ROWS = GATHER_SHARE * D_FF_STEPS
TM_OUT = 256
ATTN_CHUNK = 32
ATTN_BLOCK = 8
ROW_ALIGN = 1024
ROW_MULT = 16


def _round_up(x, m):
    return (x + m - 1) // m * m


def _rms(x, g):
    return x * lax.rsqrt(jnp.mean(x * x, axis=-1, keepdims=True) + EPS) * g


def _dot(a, b):
    return jnp.dot(a, b, preferred_element_type=F32)


def _dot_nt(a, b):
    return lax.dot_general(a, b, (((1,), (1,)), ((), ())), preferred_element_type=F32)


class _Layout:
    def __init__(self, groups):
        self.groups = groups
        self.seq_start, self.seq_tokens = [], []
        row = 0
        for b, t in groups:
            assert t % (GRID_W * NA_ROWS) == 0
            for _ in range(b):
                self.seq_start.append(row)
                self.seq_tokens.append(t)
                row += N_META + t
        self.n_valid = row
        self.rows = _round_up(row, ROW_ALIGN)
        pos = np.zeros((self.rows, 1), np.int32)
        rem = np.ones((self.rows, 1), np.int32)
        for s, t in zip(self.seq_start, self.seq_tokens):
            n = N_META + t
            pos[s:s + n, 0] = np.arange(n)
            rem[s:s + n, 0] = n - np.arange(n)
        self.pos, self.rem = pos, rem
        items = []
        self.chunk = ATTN_CHUNK
        self.kwin = self.chunk + NA_ROWS
        for s, t in zip(self.seq_start, self.seq_tokens):
            g = t // GRID_W
            assert g % self.chunk == 0
            for c in range(g // self.chunk):
                r_base = c * self.chunk
                kv_row0 = int(np.clip(r_base - NA_ROWS // 2, 0, max(g - self.kwin, 0)))
                items.append((s + N_META + r_base * GRID_W, s + N_META + kv_row0 * GRID_W,
                              s, r_base, g, kv_row0))
        self.items = np.asarray(items, np.int32)
        assert int(self.items[:, 1].max()) + self.kwin * GRID_W <= self.rows
        self.td = max(d for d in range(8, 513, 8) if self.n_valid % d == 0)
        self.slots = _round_up(2 * self.n_valid + N_EXPERTS * (TM_FFN - 1), TM_FFN)
        self.src_len = self.slots + GATHER_ROWS


def _in_proj_kernel(x_ref, g_ref, w_ref, qkv_ref, u_ref, xn_ref):
    j = pl.program_id(1)

    @pl.when(j == 0)
    def _():
        xn_ref[...] = _rms(x_ref[...], g_ref[...]).astype(BF16)

    z = _dot(xn_ref[...], w_ref[...])

    @pl.when(j == 0)
    def _():
        qkv_ref[...] = (z * (HEAD_DIM ** -0.5)).astype(BF16)

    @pl.when(jnp.logical_and(j > 0, j < 3))
    def _():
        qkv_ref[...] = z.astype(BF16)

    @pl.when(j == 3)
    def _():
        u_ref[...] = z


def _in_proj(h, g, w_bf):
    rows = h.shape[0]
    assert TN_IN == ATTN_WIDTH == POOL_WIDTH
    return pl.pallas_call(
        _in_proj_kernel,
        grid=(rows // TM_IN, 4),
        in_specs=[
            pl.BlockSpec((TM_IN, D_MODEL), lambda i, j: (i, 0)),
            pl.BlockSpec((1, D_MODEL), lambda i, j: (0, 0)),
            pl.BlockSpec((D_MODEL, TN_IN), lambda i, j: (0, j)),
        ],
        out_specs=[
            pl.BlockSpec((TM_IN, TN_IN), lambda i, j: (i, jnp.minimum(j, 2))),
            pl.BlockSpec((TM_IN, TN_IN), lambda i, j: (i, 0)),
        ],
        out_shape=[
            jax.ShapeDtypeStruct((rows, 3 * ATTN_WIDTH), BF16),
            jax.ShapeDtypeStruct((rows, POOL_WIDTH), F32),
        ],
        scratch_shapes=[pltpu.VMEM((TM_IN, D_MODEL), BF16)],
        compiler_params=pltpu.CompilerParams(
            dimension_semantics=("arbitrary", "arbitrary"), vmem_limit_bytes=VMEM_LIMIT),
        name="in_proj",
    )(h, g.reshape(1, D_MODEL), w_bf)


def _attn_kernel(qoff_ref, kvoff_ref, moff_ref, rbase_ref, gsz_ref, kvrow_ref,
                 q_ref, k_ref, v_ref, km_ref, vm_ref, bias_ref, mb_ref, o_ref,
                 ka_ref, kb_ref, kme_ref, vme_ref, s_ref, p_ref, l_ref, m_ref, *, chunk):
    it = pl.program_id(1)
    r_base = rbase_ref[it]
    g_rows = gsz_ref[it]
    kv_row0 = kvrow_ref[it]
    is_a = lax.broadcasted_iota(jnp.int32, (1, LANES), 1) < HEAD_DIM

    k = k_ref[...]
    kz = jnp.zeros_like(k)
    ka_ref[...] = jnp.where(is_a, k, kz)
    kb_ref[...] = jnp.where(is_a, kz, k)
    km = km_ref[...]
    kmz = jnp.zeros_like(km)
    kme_ref[...] = jnp.zeros_like(kme_ref)
    kme_ref[0:N_META, :] = jnp.where(is_a, km, kmz)
    kme_ref[N_META:2 * N_META, :] = jnp.where(is_a, kmz, km)
    vme_ref[...] = jnp.zeros_like(vme_ref)
    vme_ref[0:N_META, :] = vm_ref[...]
    vme_ref[N_META:2 * N_META, :] = vm_ref[...]

    nk = NA_ROWS * GRID_W
    krefs = (ka_ref, kb_ref)

    def row_window(r):
        rg = r_base + r
        rs = jnp.clip(rg - NA_ROWS // 2, 0, g_rows - NA_ROWS)
        var = rs - rg + (NA_ROWS - 1)
        kl = pl.multiple_of((rs - kv_row0) * GRID_W, GRID_W)
        q0 = pl.multiple_of(r * GRID_W, GRID_W)
        return var, kl, q0

    def block(bi, carry):
        for rr in range(ATTN_BLOCK):
            var, kl, q0 = row_window(bi * ATTN_BLOCK + rr)
            rows = slice(rr * GRID_W, (rr + 1) * GRID_W)
            q = q_ref[pl.ds(q0, GRID_W), :]
            s_meta = _dot_nt(q, kme_ref[...])
            for hd in range(HEAD_PAIR):
                sc = _dot_nt(q, krefs[hd][pl.ds(kl, nk), :]) + bias_ref[hd, var]
                sm = s_meta + mb_ref[hd]
                s_ref[hd, rows, 0:nk] = sc
                s_ref[hd, rows, nk:nk + LANES] = sm
                for c in range(nk // LANES):
                    sm = jnp.maximum(sm, sc[:, c * LANES:(c + 1) * LANES])
                m_ref[hd, rows, :] = sm
        for hd in range(HEAD_PAIR):
            for rr in range(ATTN_BLOCK):
                rows = slice(rr * GRID_W, (rr + 1) * GRID_W)
                m = jnp.max(m_ref[hd, rows, :], axis=1, keepdims=True)
                acc = None
                for c in range(nk // LANES + 1):
                    cols = slice(c * LANES, (c + 1) * LANES)
                    p = jnp.exp(s_ref[hd, rows, cols] - m)
                    acc = p if acc is None else acc + p
                    p_ref[hd, rows, cols] = p.astype(BF16)
                inv = 1.0 / jnp.sum(acc, axis=1, keepdims=True)
                l_ref[hd, rows, :] = jnp.broadcast_to(inv, (GRID_W, LANES))
        for rr in range(ATTN_BLOCK):
            var, kl, q0 = row_window(bi * ATTN_BLOCK + rr)
            rows = slice(rr * GRID_W, (rr + 1) * GRID_W)
            vv = v_ref[pl.ds(kl, nk), :]
            outs = []
            for hd in range(HEAD_PAIR):
                o = (_dot(p_ref[hd, rows, 0:nk], vv)
                     + _dot(p_ref[hd, rows, nk:nk + LANES], vme_ref[...]))
                outs.append(o * l_ref[hd, rows, :])
            o_ref[pl.ds(q0, GRID_W), :] = jnp.where(is_a, outs[0], outs[1]).astype(o_ref.dtype)
        return carry

    lax.fori_loop(0, chunk // ATTN_BLOCK, block, 0)


def _attention_grid(qkv, o_init, bias8, mbias, lay):
    rows = qkv.shape[0]
    qrows = lay.chunk * GRID_W
    krows = lay.kwin * GRID_W
    n_items = lay.items.shape[0]
    n_pairs = ATTN_HEADS // HEAD_PAIR
    tabs = [jnp.asarray(lay.items[:, c]) for c in range(6)]
    el = pl.Element

    def qmap(hp, it, qo, ko, mo, rb, gs, kr):
        return (pl.multiple_of(qo[it], ROW_MULT), pl.multiple_of(hp * LANES, LANES))

    def kmap(col0):
        def f(hp, it, qo, ko, mo, rb, gs, kr):
            return (pl.multiple_of(ko[it], ROW_MULT), pl.multiple_of(col0 + hp * LANES, LANES))
        return f

    def mmap(col0):
        def f(hp, it, qo, ko, mo, rb, gs, kr):
            return (pl.multiple_of(mo[it], ROW_MULT), pl.multiple_of(col0 + hp * LANES, LANES))
        return f

    grid_spec = pltpu.PrefetchScalarGridSpec(
        num_scalar_prefetch=6,
        grid=(n_pairs, n_items),
        in_specs=[
            pl.BlockSpec((el(qrows), el(LANES)), qmap),
            pl.BlockSpec((el(krows), el(LANES)), kmap(ATTN_WIDTH)),
            pl.BlockSpec((el(krows), el(LANES)), kmap(2 * ATTN_WIDTH)),
            pl.BlockSpec((el(N_META), el(LANES)), mmap(ATTN_WIDTH)),
            pl.BlockSpec((el(N_META), el(LANES)), mmap(2 * ATTN_WIDTH)),
            pl.BlockSpec((HEAD_PAIR, NA_ROWS, GRID_W, NA_ROWS * GRID_W),
                         lambda hp, it, *_: (hp, 0, 0, 0)),
            pl.BlockSpec((HEAD_PAIR, 1, LANES), lambda hp, it, *_: (hp, 0, 0)),
            pl.BlockSpec(memory_space=pl.ANY),
        ],
        out_specs=pl.BlockSpec((el(qrows), el(LANES)), qmap),
        scratch_shapes=[
            pltpu.VMEM((krows, LANES), BF16),
            pltpu.VMEM((krows, LANES), BF16),
            pltpu.VMEM((LANES, LANES), BF16),
            pltpu.VMEM((LANES, LANES), BF16),
            pltpu.VMEM((HEAD_PAIR, ATTN_BLOCK * GRID_W, NA_ROWS * GRID_W + LANES), F32),
            pltpu.VMEM((HEAD_PAIR, ATTN_BLOCK * GRID_W, NA_ROWS * GRID_W + LANES), BF16),
            pltpu.VMEM((HEAD_PAIR, ATTN_BLOCK * GRID_W, LANES), F32),
            pltpu.VMEM((HEAD_PAIR, ATTN_BLOCK * GRID_W, LANES), F32),
        ],
    )

    def kern(qo, ko, mo, rb, gs, kr, q, k, v, km, vm, b, mb, o_in, o, *scr):
        del o_in
        _attn_kernel(qo, ko, mo, rb, gs, kr, q, k, v, km, vm, b, mb, o, *scr, chunk=lay.chunk)

    return pl.pallas_call(
        kern,
        grid_spec=grid_spec,
        out_shape=jax.ShapeDtypeStruct((rows, ATTN_WIDTH), BF16),
        input_output_aliases={13: 0},
        compiler_params=pltpu.CompilerParams(
            dimension_semantics=("arbitrary", "arbitrary"), vmem_limit_bytes=VMEM_LIMIT),
        name="attn_grid",
    )(*tabs, qkv, qkv, qkv, qkv, qkv, bias8, mbias, o_init)


def _attn_meta_kernel(moff_ref, x_ref, mb_ref, o_in_ref, o_ref):
    del moff_ref, o_in_ref
    x = x_ref[...].astype(F32)
    outs = []
    for h in range(ATTN_HEADS):
        q = x[:, h * HEAD_DIM:(h + 1) * HEAD_DIM].astype(BF16)
        k = x[:, ATTN_WIDTH + h * HEAD_DIM:ATTN_WIDTH + (h + 1) * HEAD_DIM].astype(BF16)
        v = x[:, 2 * ATTN_WIDTH + h * HEAD_DIM:2 * ATTN_WIDTH + (h + 1) * HEAD_DIM].astype(BF16)
        s = _dot_nt(q, k) + mb_ref[h:h + 1, :]
        m = jnp.max(s, axis=1, keepdims=True)
        p = jnp.exp(s - m)
        l = jnp.sum(p, axis=1, keepdims=True)
        outs.append(_dot(p.astype(BF16), v) / l)
    o_ref[...] = jnp.concatenate(outs, axis=1).astype(o_ref.dtype)


def _attention_meta(qkv, o_init, meta_bias, lay):
    rows = qkv.shape[0]
    n_seq = len(lay.seq_start)
    moff = jnp.asarray(np.asarray(lay.seq_start, np.int32))
    el = pl.Element
    grid_spec = pltpu.PrefetchScalarGridSpec(
        num_scalar_prefetch=1,
        grid=(n_seq,),
        in_specs=[
            pl.BlockSpec((el(N_META), el(3 * ATTN_WIDTH)),
                         lambda b, mo: (pl.multiple_of(mo[b], ROW_MULT), 0)),
            pl.BlockSpec((ATTN_HEADS, N_META), lambda b, mo: (0, 0)),
            pl.BlockSpec(memory_space=pl.ANY),
        ],
        out_specs=pl.BlockSpec((el(N_META), el(ATTN_WIDTH)),
                               lambda b, mo: (pl.multiple_of(mo[b], ROW_MULT), 0)),
    )
    return pl.pallas_call(
        _attn_meta_kernel,
        grid_spec=grid_spec,
        out_shape=jax.ShapeDtypeStruct((rows, ATTN_WIDTH), BF16),
        input_output_aliases={3: 0},
        compiler_params=pltpu.CompilerParams(dimension_semantics=("arbitrary",)),
        name="attn_meta",
    )(moff, qkv, meta_bias, o_init)


def _attn_bias_tables(rpb, meta_bias):
    cq = np.arange(GRID_W)
    cs = np.clip(cq - NA_COLS // 2, 0, GRID_W - NA_COLS)
    ck = np.arange(GRID_W)
    valid = (ck[None, :] >= cs[:, None]) & (ck[None, :] < cs[:, None] + NA_COLS)
    dc = ck[None, :] - cq[:, None] + NA_COLS - 1
    onehot = (np.arange(2 * NA_COLS - 1)[:, None, None] == dc[None]) & valid[None]
    t = jnp.einsum('hrd,dqk->hqrk', rpb.astype(F32), jnp.asarray(onehot, F32),
                   precision=lax.Precision.HIGHEST)
    t = t + jnp.asarray(np.where(valid, 0.0, NEG), F32)[None, :, None, :]
    bias8 = jnp.stack([t[:, :, v:v + NA_ROWS, :].reshape(ATTN_HEADS, GRID_W, NA_ROWS * GRID_W)
                       for v in range(NA_ROWS)], axis=1)
    lane = np.arange(LANES)
    head = np.arange(ATTN_HEADS)
    slot = (head % HEAD_PAIR) * N_META
    src = np.clip(lane[None, :] - slot[:, None], 0, N_META - 1)
    ok = (lane[None, :] >= slot[:, None]) & (lane[None, :] < slot[:, None] + N_META)
    mb = jnp.where(jnp.asarray(ok), jnp.take_along_axis(meta_bias.astype(F32), jnp.asarray(src), 1), NEG)
    return bias8, mb.reshape(ATTN_HEADS, 1, LANES)


def _mix_out_kernel(*refs, moe, n_valid):
    if moe:
        (oa_ref, u_ref, up_ref, un_ref, pos_ref, rem_ref, pw_ref, ps_ref, ga_ref, gp_ref, wo_ref,
         h_ref, lf_ref, rt_ref, tri_ref, h1_ref, hn_ref, route_ref, cnt_ref, ext_ref, run_ref) = refs
    else:
        (oa_ref, u_ref, up_ref, un_ref, pos_ref, rem_ref, pw_ref, ps_ref, ga_ref, gp_ref, wo_ref,
         h_ref, lf_ref, h1_ref, hn_ref, ext_ref) = refs
    tm = u_ref.shape[0]
    hal = POOL_HALO
    ext_ref[0:hal, :] = up_ref[...]
    ext_ref[hal:hal + tm, :] = u_ref[...]
    ext_ref[hal + tm:hal + tm + hal, :] = un_ref[...]
    pos = pos_ref[...]
    rem = rem_ref[...]

    mixed = []
    for g, w in enumerate(POOL_WINDOWS):
        half = w // 2
        c0 = g * POOL_GROUP
        acc = ext_ref[hal:hal + tm, c0:c0 + POOL_GROUP]
        centre = acc
        for kk in range(-half, half):
            if kk == 0:
                continue
            sl = ext_ref[hal + kk:hal + kk + tm, c0:c0 + POOL_GROUP]
            ok = (pos >= -kk) if kk < 0 else (rem > kk)
            acc = acc + jnp.where(ok, sl, 0.0)
        cnt = (jnp.minimum(pos, half) + jnp.minimum(rem, half)).astype(F32)
        pooled = (acc / cnt - centre).astype(BF16)
        mixed.append(_dot(pooled, pw_ref[g]))
    o_pool = jnp.concatenate(mixed, axis=1) * ps_ref[...]
    n_pool = _rms(o_pool, gp_ref[...]).astype(BF16)
    n_attn = _rms(oa_ref[...].astype(F32), ga_ref[...]).astype(BF16)
    mix = _dot(n_attn, wo_ref[0:ATTN_WIDTH, :]) + _dot(n_pool, wo_ref[ATTN_WIDTH:D_MODEL, :])
    h1 = h_ref[...] + mix
    h1_ref[...] = h1
    hn = _rms(h1, lf_ref[...])
    if not moe:
        hn_ref[...] = hn.astype(BF16)
        return

    hn_ref[...] = hn
    i = pl.program_id(0)

    @pl.when(i == 0)
    def _():
        run_ref[...] = jnp.zeros_like(run_ref)

    hi = hn.astype(BF16)
    lo = (hn - hi.astype(F32)).astype(BF16)
    l1 = _dot_nt(rt_ref[...], hi)
    l2 = _dot_nt(rt_ref[0:16, :], lo)
    lg = l1[0:N_EXPERTS] + l1[16:16 + N_EXPERTS] + l2[0:N_EXPERTS]
    eidx = lax.broadcasted_iota(jnp.int32, (N_EXPERTS, tm), 0).astype(F32)
    none = float(N_EXPERTS)
    m1 = jnp.max(lg, axis=0, keepdims=True)
    i1 = jnp.min(jnp.where(lg == m1, eidx, none), axis=0, keepdims=True)
    sel1 = eidx == i1
    lg2 = jnp.where(sel1, -jnp.inf, lg)
    m2 = jnp.max(lg2, axis=0, keepdims=True)
    i2 = jnp.min(jnp.where(lg2 == m2, eidx, none), axis=0, keepdims=True)
    sel2 = eidx == i2
    t = jnp.exp(m2 - m1)
    g1 = 1.0 / (1.0 + t)
    g2 = t / (1.0 + t)
    rowid = i * tm + lax.broadcasted_iota(jnp.int32, (1, tm), 1)
    valid = rowid < n_valid
    c = jnp.where(jnp.logical_and(jnp.logical_or(sel1, sel2), valid), 1.0, 0.0)
    c16 = jnp.concatenate([c, jnp.zeros_like(c)], axis=0).astype(BF16)
    cs = _dot(c16, tri_ref[...])[0:N_EXPERTS]
    rank_all = run_ref[:, 0:1] + cs - 1.0
    rank1 = jnp.sum(jnp.where(sel1, rank_all, 0.0), axis=0, keepdims=True)
    rank2 = jnp.sum(jnp.where(sel2, rank_all, 0.0), axis=0, keepdims=True)
    run_new = run_ref[...] + jnp.sum(c, axis=1, keepdims=True)
    run_ref[...] = run_new
    cnt_ref[...] = run_new
    route = jnp.zeros((N_EXPERTS, tm), F32)
    for k, row in enumerate((i1, i2, g1, g2, rank1, rank2)):
        route = jnp.where(eidx == float(k), row, route)
    route_ref[...] = route


def _mix_out(o_attn, u, h, lay, pool_w_bf, pool_scale, g_attn, g_pool, w_out_bf, ln_ffn,
             router_t=None):
    rows = h.shape[0]
    tm = TM_MIX
    moe = router_t is not None
    blk8 = tm // POOL_HALO
    n8 = rows // POOL_HALO
    row1 = lambda a: a.reshape(1, -1)
    in_specs = [
        pl.BlockSpec((tm, ATTN_WIDTH), lambda i: (i, 0)),
        pl.BlockSpec((tm, POOL_WIDTH), lambda i: (i, 0)),
        pl.BlockSpec((POOL_HALO, POOL_WIDTH), lambda i: (jnp.maximum(i * blk8 - 1, 0), 0)),
        pl.BlockSpec((POOL_HALO, POOL_WIDTH), lambda i: (jnp.minimum((i + 1) * blk8, n8 - 1), 0)),
        pl.BlockSpec((tm, 1), lambda i: (i, 0)),
        pl.BlockSpec((tm, 1), lambda i: (i, 0)),
        pl.BlockSpec((len(POOL_WINDOWS), POOL_GROUP, POOL_GROUP), lambda i: (0, 0, 0)),
        pl.BlockSpec((1, POOL_WIDTH), lambda i: (0, 0)),
        pl.BlockSpec((1, ATTN_WIDTH), lambda i: (0, 0)),
        pl.BlockSpec((1, POOL_WIDTH), lambda i: (0, 0)),
        pl.BlockSpec((D_MODEL, D_MODEL), lambda i: (0, 0), pipeline_mode=pl.Buffered(1)),
        pl.BlockSpec((tm, D_MODEL), lambda i: (i, 0)),
        pl.BlockSpec((1, D_MODEL), lambda i: (0, 0)),
    ]
    args = [o_attn, u, u, u, jnp.asarray(lay.pos), jnp.asarray(lay.rem), pool_w_bf,
            row1(pool_scale), row1(g_attn), row1(g_pool), w_out_bf, h, row1(ln_ffn)]
    out_specs = [pl.BlockSpec((tm, D_MODEL), lambda i: (i, 0)),
                 pl.BlockSpec((tm, D_MODEL), lambda i: (i, 0))]
    out_shape = [jax.ShapeDtypeStruct((rows, D_MODEL), F32),
                 jax.ShapeDtypeStruct((rows, D_MODEL), F32 if moe else BF16)]
    scratch = [pltpu.VMEM((tm + 2 * POOL_HALO, POOL_WIDTH), F32)]
    if moe:
        tri = jnp.asarray(np.triu(np.ones((tm, tm), np.float32)), BF16)
        in_specs += [pl.BlockSpec((32, D_MODEL), lambda i: (0, 0)),
                     pl.BlockSpec((tm, tm), lambda i: (0, 0))]
        args += [router_t, tri]
        out_specs += [pl.BlockSpec((N_EXPERTS, tm), lambda i: (0, i)),
                      pl.BlockSpec((N_EXPERTS, LANES), lambda i: (0, 0))]
        out_shape += [jax.ShapeDtypeStruct((N_EXPERTS, rows), F32),
                      jax.ShapeDtypeStruct((N_EXPERTS, LANES), F32)]
        scratch += [pltpu.VMEM((N_EXPERTS, LANES), F32)]
    return pl.pallas_call(
        functools.partial(_mix_out_kernel, moe=moe, n_valid=lay.n_valid),
        grid=(rows // tm,),
        in_specs=in_specs,
        out_specs=out_specs,
        out_shape=out_shape,
        scratch_shapes=scratch,
        compiler_params=pltpu.CompilerParams(
            dimension_semantics=("arbitrary",), vmem_limit_bytes=VMEM_LIMIT),
        name="mix_out_moe" if moe else "mix_out",
    )(*args)


def _swiglu_step(x, wg_ref, wu_ref, wd_ref):
    gate = _dot(x, wg_ref[...])
    up = _dot(x, wu_ref[...])
    mid = (gate / (1.0 + jnp.exp(-gate))) * up
    return _dot(mid.astype(BF16), wd_ref[...])


def _ffn_dense_kernel(x_ref, wg_ref, wu_ref, wd_ref, res_ref, o_ref):
    @pl.when(pl.program_id(1) == 0)
    def _():
        o_ref[...] = res_ref[...]

    o_ref[...] += _swiglu_step(x_ref[...], wg_ref, wu_ref, wd_ref)


def _ffn_dense(x, wg_bf, wu_bf, wd_bf, res):
    rows = x.shape[0]
    tm, tf = TM_FFN, TF_FFN
    return pl.pallas_call(
        _ffn_dense_kernel,
        grid=(rows // tm, D_FF // tf),
        in_specs=[
            pl.BlockSpec((tm, D_MODEL), lambda i, j: (i, 0)),
            pl.BlockSpec((D_MODEL, tf), lambda i, j: (0, j)),
            pl.BlockSpec((D_MODEL, tf), lambda i, j: (0, j)),
            pl.BlockSpec((tf, D_MODEL), lambda i, j: (j, 0)),
            pl.BlockSpec((tm, D_MODEL), lambda i, j: (i, 0)),
        ],
        out_specs=pl.BlockSpec((tm, D_MODEL), lambda i, j: (i, 0)),
        out_shape=jax.ShapeDtypeStruct((rows, D_MODEL), F32),
        compiler_params=pltpu.CompilerParams(
            dimension_semantics=("arbitrary", "arbitrary"), vmem_limit_bytes=VMEM_LIMIT),
        name="ffn_dense",
    )(x, wg_bf, wu_bf, wd_bf, res)


def _ffn_moe_kernel(te_ref, nu_ref, src_ref, hn_ref, wg_ref, wu_ref, wd_ref, o_ref,
                    xbuf_ref, xb_ref, sem):
    del te_ref
    tm = o_ref.shape[0]
    n_tiles = pl.num_programs(0)
    nf = pl.num_programs(1)
    share = xbuf_ref.shape[1] // D_FF_STEPS
    i = pl.program_id(0)
    j = pl.program_id(1)
    n_used = nu_ref[0]
    used = i < n_used
    slot = lax.rem(i, 2)

    def row_copy(tile, row, buf):
        return pltpu.make_async_copy(hn_ref.at[pl.ds(src_ref[tile * tm + row], 1)],
                                     xbuf_ref.at[buf, pl.ds(row, 1)], sem.at[buf])

    def tile_rows(buf):
        return pltpu.make_async_copy(hn_ref.at[pl.ds(0, xbuf_ref.shape[1])], xbuf_ref.at[buf],
                                     sem.at[buf])

    @pl.when(j == 0)
    def _():
        @pl.when(i == 0)
        def _():
            def body(r, c):
                row_copy(0, r, 0).start()
                return c

            lax.fori_loop(0, xbuf_ref.shape[1], body, 0, unroll=8)

        @pl.when(i <= n_used)
        def _():
            tile_rows(slot).wait()

        @pl.when(used)
        def _():
            xb_ref[...] = xbuf_ref[slot, 0:tm, :].astype(BF16)

        o_ref[...] = jnp.zeros_like(o_ref)

    @pl.when(used)
    def _():
        for k in range(share):
            row_copy(i + 1, j * share + k, 1 - slot).start()
        o_ref[...] += _swiglu_step(xb_ref[...], wg_ref, wu_ref, wd_ref)

    @pl.when(jnp.logical_and(used, jnp.logical_and(i == n_tiles - 1, j == nf - 1)))
    def _():
        tile_rows(1 - slot).wait()


def _ffn_moe(hn, src, wg_bf, wu_bf, wd_bf, tile_expert, n_used, lay):
    tm, tf = TM_FFN, TF_FFN
    nf = D_FF // tf

    def jj(i, j, nu):
        return jnp.where(i < nu[0], j, nf - 1)

    grid_spec = pltpu.PrefetchScalarGridSpec(
        num_scalar_prefetch=3,
        grid=(lay.slots // tm, nf),
        in_specs=[
            pl.BlockSpec(memory_space=pl.ANY),
            pl.BlockSpec((None, D_MODEL, tf), lambda i, j, te, nu, sr: (te[i], 0, jj(i, j, nu))),
            pl.BlockSpec((None, D_MODEL, tf), lambda i, j, te, nu, sr: (te[i], 0, jj(i, j, nu))),
            pl.BlockSpec((None, tf, D_MODEL), lambda i, j, te, nu, sr: (te[i], jj(i, j, nu), 0)),
        ],
        out_specs=pl.BlockSpec((tm, D_MODEL), lambda i, j, te, nu, sr: (i, 0)),
        scratch_shapes=[
            pltpu.VMEM((2, GATHER_ROWS, D_MODEL), F32),
            pltpu.VMEM((tm, D_MODEL), BF16),
            pltpu.SemaphoreType.DMA((2,)),
        ],
    )
    return pl.pallas_call(
        _ffn_moe_kernel,
        grid_spec=grid_spec,
        out_shape=jax.ShapeDtypeStruct((lay.slots, D_MODEL), F32),
        compiler_params=pltpu.CompilerParams(
            dimension_semantics=("arbitrary", "arbitrary"), vmem_limit_bytes=VMEM_LIMIT),
        name="ffn_moe",
    )(tile_expert, n_used, src, hn, wg_bf, wu_bf, wd_bf)


def _slot_src_kernel(pos_ref, src_ref, *, td, slots):
    i = pl.program_id(0)

    @pl.when(i == 0)
    def _():
        def init(s, c):
            src_ref[s] = 0
            return c

        lax.fori_loop(0, slots, init, 0, unroll=8)

    base = i * td

    def body(r, c):
        src_ref[pos_ref[0, r]] = base + r
        src_ref[pos_ref[1, r]] = base + r
        return c

    lax.fori_loop(0, td, body, 0, unroll=8)


def _slot_sources(pos3, lay):
    td = lay.td
    return pl.pallas_call(
        functools.partial(_slot_src_kernel, td=td, slots=lay.src_len),
        grid=(lay.n_valid // td,),
        in_specs=[pl.BlockSpec((None, 2, td), lambda i: (i, 0, 0), memory_space=pltpu.SMEM)],
        out_specs=pl.BlockSpec(memory_space=pltpu.SMEM),
        out_shape=jax.ShapeDtypeStruct((lay.src_len,), jnp.int32),
        compiler_params=pltpu.CompilerParams(dimension_semantics=("arbitrary",)),
        name="slot_sources",
    )(pos3)


def _combine_kernel(pos_ref, gate_ref, h_ref, gf_ref, y_ref, o_ref, buf_ref, sem, *, tc):
    def issue(r, c):
        for k in range(2):
            pltpu.make_async_copy(y_ref.at[pl.ds(pos_ref[k, r], 1)],
                                  buf_ref.at[k, pl.ds(r, 1)], sem).start()
        return c

    lax.fori_loop(0, tc, issue, 0)
    for k in range(2):
        pltpu.make_async_copy(y_ref.at[pl.ds(0, tc)], buf_ref.at[k], sem).wait()
    gate = gate_ref[...]
    y = gate[:, 0:1] * buf_ref[0] + gate[:, 1:2] * buf_ref[1]
    o_ref[...] = _rms(h_ref[...] + y, gf_ref[...]).astype(o_ref.dtype)


def _combine(h1, y_sorted, pos, gates, g_final, row0, batch, tokens):
    tc = TM_OUT
    per = tokens // tc
    el = pl.Element
    seq = N_META + tokens

    def grid_rows(a):
        a = a[:, row0:row0 + batch * seq].reshape(a.shape[0], batch, seq)[:, :, N_META:]
        return a.reshape(a.shape[0], batch * tokens)

    pos_g = grid_rows(pos).reshape(2, batch * per, tc).transpose(1, 0, 2)
    gates_g = grid_rows(gates).T

    def rmap(t):
        return (pl.multiple_of(row0 + N_META + (t // per) * seq + (t % per) * tc, ROW_MULT), 0)

    return pl.pallas_call(
        functools.partial(_combine_kernel, tc=tc),
        grid=(batch * per,),
        in_specs=[
            pl.BlockSpec((None, 2, tc), lambda t: (t, 0, 0), memory_space=pltpu.SMEM),
            pl.BlockSpec((tc, 2), lambda t: (t, 0)),
            pl.BlockSpec((el(tc), el(D_MODEL)), rmap),
            pl.BlockSpec((1, D_MODEL), lambda t: (0, 0)),
            pl.BlockSpec(memory_space=pl.ANY),
        ],
        out_specs=pl.BlockSpec((None, tc, D_MODEL), lambda t: (t // per, t % per, 0)),
        out_shape=jax.ShapeDtypeStruct((batch, tokens, D_MODEL), F32),
        scratch_shapes=[pltpu.VMEM((2, tc, D_MODEL), F32), pltpu.SemaphoreType.DMA(())],
        compiler_params=pltpu.CompilerParams(
            dimension_semantics=("arbitrary",), vmem_limit_bytes=VMEM_LIMIT),
        name="combine",
    )(pos_g, gates_g, h1, g_final.reshape(1, D_MODEL), y_sorted)


def _routing_tables(route, cnt, lay):
    tmx = TM_FFN
    counts = cnt[:, 0].astype(jnp.int32)
    padded = (counts + tmx - 1) // tmx * tmx
    ends = jnp.cumsum(padded)
    offs = ends - padded
    e_idx = route[0:2].astype(jnp.int32)
    rank = route[4:6].astype(jnp.int32)
    onehot = e_idx[:, :, None] == jnp.arange(N_EXPERTS, dtype=jnp.int32)
    pos = jnp.sum(jnp.where(onehot, offs, 0), axis=-1) + rank
    pos = jnp.where(jnp.arange(pos.shape[1])[None, :] < lay.n_valid, pos, 0)
    n_tiles = lay.slots // tmx
    tile_start = jnp.arange(n_tiles, dtype=jnp.int32) * tmx
    tile_expert = jnp.minimum(jnp.sum(tile_start[:, None] >= ends[None, :], axis=1),
                              N_EXPERTS - 1).astype(jnp.int32)
    n_used = (ends[-1] // tmx).reshape(1).astype(jnp.int32)
    return pos, route[2:4], tile_expert, n_used


def _forward(xs, meta_tokens, ln_mix, w_in, rpb, meta_bias, pool_w, pool_scale, g_attn_out,
             g_pool_out, w_out, ln_ffn, ffn_w_gate, ffn_w_up, ffn_w_down, router, moe_w_gate,
             moe_w_up, moe_w_down, g_final):
    groups = [(x.shape[0], x.shape[1]) for x in xs]
    lay = _Layout(groups)
    depth = ln_mix.shape[0]
    assert depth == 2

    parts = []
    for x in xs:
        for b in range(x.shape[0]):
            parts += [meta_tokens.astype(F32), x[b]]
    parts.append(jnp.zeros((lay.rows - lay.n_valid, D_MODEL), F32))
    h = jnp.concatenate(parts, axis=0)

    outs = None
    for layer in range(depth):
        qkv, u = _in_proj(h, ln_mix[layer], w_in[layer].astype(BF16))
        bias8, mbias = _attn_bias_tables(rpb[layer], meta_bias[layer])
        o_attn = jnp.zeros((lay.rows, ATTN_WIDTH), BF16)
        o_attn = _attention_grid(qkv, o_attn, bias8, mbias, lay)
        o_attn = _attention_meta(qkv, o_attn, meta_bias[layer].astype(F32), lay)
        mix_args = (o_attn, u, h, lay, pool_w[layer].astype(BF16), pool_scale[layer],
                    g_attn_out[layer], g_pool_out[layer], w_out[layer].astype(BF16), ln_ffn[layer])
        i = layer // 2
        if layer % 2 == 0:
            h1, hn = _mix_out(*mix_args)
            h = _ffn_dense(hn, ffn_w_gate[i].astype(BF16), ffn_w_up[i].astype(BF16),
                           ffn_w_down[i].astype(BF16), h1)
        else:
            rt = router[i].T
            rt_hi = rt.astype(BF16)
            rt_lo = (rt - rt_hi.astype(F32)).astype(BF16)
            zpad = jnp.zeros_like(rt_hi)
            router_t = jnp.concatenate([rt_hi, zpad, rt_lo, zpad], axis=0)
            h1, hn, route, cnt = _mix_out(*mix_args, router_t=router_t)
            pos, gates, tile_expert, n_used = _routing_tables(route, cnt, lay)
            td = lay.td
            pos3 = pos[:, :lay.n_valid].reshape(2, lay.n_valid // td, td).transpose(1, 0, 2)
            src = _slot_sources(pos3, lay)
            y_sorted = _ffn_moe(hn, src, moe_w_gate[i].astype(BF16), moe_w_up[i].astype(BF16),
                                moe_w_down[i].astype(BF16), tile_expert, n_used, lay)
            outs = []
            seq = 0
            for b, t in groups:
                outs.append(_combine(h1, y_sorted, pos, gates, g_final, lay.seq_start[seq], b, t))
                seq += b
    return tuple(outs)


def kernel(x_prompt, x_sample, meta_tokens, ln_mix, w_in, rpb, meta_bias, pool_w, pool_scale,
           g_attn_out, g_pool_out, w_out, ln_ffn, ffn_w_gate, ffn_w_up, ffn_w_down, router,
           moe_w_gate, moe_w_up, moe_w_down, g_final):
    return _forward((x_prompt, x_sample), meta_tokens, ln_mix, w_in, rpb, meta_bias, pool_w,
                    pool_scale, g_attn_out, g_pool_out, w_out, ln_ffn, ffn_w_gate, ffn_w_up,
                    ffn_w_down, router, moe_w_gate, moe_w_up, moe_w_down, g_final)
```

```python
import functools

import numpy as np
import jax
import jax.numpy as jnp
from jax import lax
from jax.experimental import pallas as pl
from jax.experimental.pallas import tpu as pltpu

F32 = jnp.float32
BF16 = jnp.bfloat16

D_MODEL = 2048
GRID_W = 64
N_META = 16
ATTN_HEADS = 16
HEAD_DIM = 64
ATTN_WIDTH = ATTN_HEADS * HEAD_DIM
POOL_WIDTH = D_MODEL - ATTN_WIDTH
POOL_WINDOWS = (2, 4, 8, 16)
POOL_GROUP = POOL_WIDTH // len(POOL_WINDOWS)
POOL_HALO = max(POOL_WINDOWS) // 2
NA_ROWS = 8
NA_COLS = 16
D_FF = 5632
N_EXPERTS = 8
EPS = 1e-6
NEG = -1e30

LANES = 128
HEAD_PAIR = LANES // HEAD_DIM
VMEM_LIMIT = 56 * 1024 * 1024

TM_IN = 1024
TN_IN = 1024
TM_MIX = 512
MIX_SUB = 512
POOL_SUB = 128
TM_FFN = 512
TF_FFN = 512
D_FF_STEPS = D_FF // TF_FFN
GATHER_SHARE = (-(-TM_FFN // D_FF_STEPS) + 7) // 8 * 8
GATHER_ROWS = GATHER_SHARE * D_FF_STEPS
TM_OUT = 256
ATTN_CHUNK = 32
ATTN_BLOCK = 8
ROW_ALIGN = 1024
ROW_MULT = 16


def _round_up(x, m):
    return (x + m - 1) // m * m


def _rms(x, g):
    return x * lax.rsqrt(jnp.mean(x * x, axis=-1, keepdims=True) + EPS) * g


def _dot(a, b):
    return jnp.dot(a, b, preferred_element_type=F32)


def _dot_nt(a, b):
    return lax.dot_general(a, b, (((1,), (1,)), ((), ())), preferred_element_type=F32)


class _Layout:
    def __init__(self, groups):
        self.groups = groups
        self.seq_start, self.seq_tokens = [], []
        row = 0
        for b, t in groups:
            assert t % (GRID_W * NA_ROWS) == 0
            for _ in range(b):
                self.seq_start.append(row)
                self.seq_tokens.append(t)
                row += N_META + t
        self.n_valid = row
        self.rows = _round_up(row, ROW_ALIGN)
        pos = np.zeros((self.rows, 1), np.int32)
        rem = np.ones((self.rows, 1), np.int32)
        for s, t in zip(self.seq_start, self.seq_tokens):
            n = N_META + t
            pos[s:s + n, 0] = np.arange(n)
            rem[s:s + n, 0] = n - np.arange(n)
        self.pos, self.rem = pos, rem
        items = []
        self.chunk = ATTN_CHUNK
        self.kwin = self.chunk + NA_ROWS
        for s, t in zip(self.seq_start, self.seq_tokens):
            g = t // GRID_W
            assert g % self.chunk == 0
            for c in range(g // self.chunk):
                r_base = c * self.chunk
                kv_row0 = int(np.clip(r_base - NA_ROWS // 2, 0, max(g - self.kwin, 0)))
                items.append((s + N_META + r_base * GRID_W, s + N_META + kv_row0 * GRID_W,
                              s, r_base, g, kv_row0))
        self.items = np.asarray(items, np.int32)
        assert int(self.items[:, 1].max()) + self.kwin * GRID_W <= self.rows
        self.td = max(d for d in range(8, 513, 8) if self.n_valid % d == 0)
        self.slots = _round_up(2 * self.n_valid + N_EXPERTS * (TM_FFN - 1), TM_FFN)
        self.src_len = self.slots + GATHER_ROWS


def _in_proj_kernel(x_ref, g_ref, w_ref, qkv_ref, u_ref, xn_ref):
    j = pl.program_id(1)

    @pl.when(j == 0)
    def _():
        xn_ref[...] = _rms(x_ref[...], g_ref[...]).astype(BF16)

    z = _dot(xn_ref[...], w_ref[...])

    @pl.when(j == 0)
    def _():
        qkv_ref[...] = (z * (HEAD_DIM ** -0.5)).astype(BF16)

    @pl.when(jnp.logical_and(j > 0, j < 3))
    def _():
        qkv_ref[...] = z.astype(BF16)

    @pl.when(j == 3)
    def _():
        u_ref[...] = z


def _in_proj(h, g, w_bf):
    rows = h.shape[0]
    assert TN_IN == ATTN_WIDTH == POOL_WIDTH
    return pl.pallas_call(
        _in_proj_kernel,
        grid=(rows // TM_IN, 4),
        in_specs=[
            pl.BlockSpec((TM_IN, D_MODEL), lambda i, j: (i, 0)),
            pl.BlockSpec((1, D_MODEL), lambda i, j: (0, 0)),
            pl.BlockSpec((D_MODEL, TN_IN), lambda i, j: (0, j)),
        ],
        out_specs=[
            pl.BlockSpec((TM_IN, TN_IN), lambda i, j: (i, jnp.minimum(j, 2))),
            pl.BlockSpec((TM_IN, TN_IN), lambda i, j: (i, 0)),
        ],
        out_shape=[
            jax.ShapeDtypeStruct((rows, 3 * ATTN_WIDTH), BF16),
            jax.ShapeDtypeStruct((rows, POOL_WIDTH), F32),
        ],
        scratch_shapes=[pltpu.VMEM((TM_IN, D_MODEL), BF16)],
        compiler_params=pltpu.CompilerParams(
            dimension_semantics=("arbitrary", "arbitrary"), vmem_limit_bytes=VMEM_LIMIT),
        name="in_proj",
    )(h, g.reshape(1, D_MODEL), w_bf)


def _attn_kernel(qoff_ref, kvoff_ref, moff_ref, rbase_ref, gsz_ref, kvrow_ref,
                 q_ref, k_ref, v_ref, km_ref, vm_ref, bias_ref, mb_ref, o_ref,
                 ka_ref, kb_ref, kme_ref, vme_ref, s_ref, p_ref, l_ref, m_ref, *, chunk):
    it = pl.program_id(1)
    r_base = rbase_ref[it]
    g_rows = gsz_ref[it]
    kv_row0 = kvrow_ref[it]
    is_a = lax.broadcasted_iota(jnp.int32, (1, LANES), 1) < HEAD_DIM

    k = k_ref[...]
    kz = jnp.zeros_like(k)
    ka_ref[...] = jnp.where(is_a, k, kz)
    kb_ref[...] = jnp.where(is_a, kz, k)
    km = km_ref[...]
    kmz = jnp.zeros_like(km)
    kme_ref[...] = jnp.zeros_like(kme_ref)
    kme_ref[0:N_META, :] = jnp.where(is_a, km, kmz)
    kme_ref[N_META:2 * N_META, :] = jnp.where(is_a, kmz, km)
    vme_ref[...] = jnp.zeros_like(vme_ref)
    vme_ref[0:N_META, :] = vm_ref[...]
    vme_ref[N_META:2 * N_META, :] = vm_ref[...]

    nk = NA_ROWS * GRID_W
    krefs = (ka_ref, kb_ref)

    def row_window(r):
        rg = r_base + r
        rs = jnp.clip(rg - NA_ROWS // 2, 0, g_rows - NA_ROWS)
        var = rs - rg + (NA_ROWS - 1)
        kl = pl.multiple_of((rs - kv_row0) * GRID_W, GRID_W)
        q0 = pl.multiple_of(r * GRID_W, GRID_W)
        return var, kl, q0

    def block(bi, carry):
        for rr in range(ATTN_BLOCK):
            var, kl, q0 = row_window(bi * ATTN_BLOCK + rr)
            rows = slice(rr * GRID_W, (rr + 1) * GRID_W)
            q = q_ref[pl.ds(q0, GRID_W), :]
            s_meta = _dot_nt(q, kme_ref[...])
            for hd in range(HEAD_PAIR):
                sc = _dot_nt(q, krefs[hd][pl.ds(kl, nk), :]) + bias_ref[hd, var]
                sm = s_meta + mb_ref[hd]
                s_ref[hd, rows, 0:nk] = sc
                s_ref[hd, rows, nk:nk + LANES] = sm
                for c in range(nk // LANES):
                    sm = jnp.maximum(sm, sc[:, c * LANES:(c + 1) * LANES])
                m_ref[hd, rows, :] = sm
        for hd in range(HEAD_PAIR):
            for rr in range(ATTN_BLOCK):
                rows = slice(rr * GRID_W, (rr + 1) * GRID_W)
                m = jnp.max(m_ref[hd, rows, :], axis=1, keepdims=True)
                acc = None
                for c in range(nk // LANES + 1):
                    cols = slice(c * LANES, (c + 1) * LANES)
                    p = jnp.exp(s_ref[hd, rows, cols] - m)
                    acc = p if acc is None else acc + p
                    p_ref[hd, rows, cols] = p.astype(BF16)
                inv = 1.0 / jnp.sum(acc, axis=1, keepdims=True)
                l_ref[hd, rows, :] = jnp.broadcast_to(inv, (GRID_W, LANES))
        for rr in range(ATTN_BLOCK):
            var, kl, q0 = row_window(bi * ATTN_BLOCK + rr)
            rows = slice(rr * GRID_W, (rr + 1) * GRID_W)
            vv = v_ref[pl.ds(kl, nk), :]
            outs = []
            for hd in range(HEAD_PAIR):
                o = (_dot(p_ref[hd, rows, 0:nk], vv)
                     + _dot(p_ref[hd, rows, nk:nk + LANES], vme_ref[...]))
                outs.append(o * l_ref[hd, rows, :])
            o_ref[pl.ds(q0, GRID_W), :] = jnp.where(is_a, outs[0], outs[1]).astype(o_ref.dtype)
        return carry

    lax.fori_loop(0, chunk // ATTN_BLOCK, block, 0)


def _attention_grid(qkv, o_init, bias8, mbias, lay):
    rows = qkv.shape[0]
    qrows = lay.chunk * GRID_W
    krows = lay.kwin * GRID_W
    n_items = lay.items.shape[0]
    n_pairs = ATTN_HEADS // HEAD_PAIR
    tabs = [jnp.asarray(lay.items[:, c]) for c in range(6)]
    el = pl.Element

    def qmap(hp, it, qo, ko, mo, rb, gs, kr):
        return (pl.multiple_of(qo[it], ROW_MULT), pl.multiple_of(hp * LANES, LANES))

    def kmap(col0):
        def f(hp, it, qo, ko, mo, rb, gs, kr):
            return (pl.multiple_of(ko[it], ROW_MULT), pl.multiple_of(col0 + hp * LANES, LANES))
        return f

    def mmap(col0):
        def f(hp, it, qo, ko, mo, rb, gs, kr):
            return (pl.multiple_of(mo[it], ROW_MULT), pl.multiple_of(col0 + hp * LANES, LANES))
        return f

    grid_spec = pltpu.PrefetchScalarGridSpec(
        num_scalar_prefetch=6,
        grid=(n_pairs, n_items),
        in_specs=[
            pl.BlockSpec((el(qrows), el(LANES)), qmap),
            pl.BlockSpec((el(krows), el(LANES)), kmap(ATTN_WIDTH)),
            pl.BlockSpec((el(krows), el(LANES)), kmap(2 * ATTN_WIDTH)),
            pl.BlockSpec((el(N_META), el(LANES)), mmap(ATTN_WIDTH)),
            pl.BlockSpec((el(N_META), el(LANES)), mmap(2 * ATTN_WIDTH)),
            pl.BlockSpec((HEAD_PAIR, NA_ROWS, GRID_W, NA_ROWS * GRID_W),
                         lambda hp, it, *_: (hp, 0, 0, 0)),
            pl.BlockSpec((HEAD_PAIR, 1, LANES), lambda hp, it, *_: (hp, 0, 0)),
            pl.BlockSpec(memory_space=pl.ANY),
        ],
        out_specs=pl.BlockSpec((el(qrows), el(LANES)), qmap),
        scratch_shapes=[
            pltpu.VMEM((krows, LANES), BF16),
            pltpu.VMEM((krows, LANES), BF16),
            pltpu.VMEM((LANES, LANES), BF16),
            pltpu.VMEM((LANES, LANES), BF16),
            pltpu.VMEM((HEAD_PAIR, ATTN_BLOCK * GRID_W, NA_ROWS * GRID_W + LANES), F32),
            pltpu.VMEM((HEAD_PAIR, ATTN_BLOCK * GRID_W, NA_ROWS * GRID_W + LANES), BF16),
            pltpu.VMEM((HEAD_PAIR, ATTN_BLOCK * GRID_W, LANES), F32),
            pltpu.VMEM((HEAD_PAIR, ATTN_BLOCK * GRID_W, LANES), F32),
        ],
    )

    def kern(qo, ko, mo, rb, gs, kr, q, k, v, km, vm, b, mb, o_in, o, *scr):
        del o_in
        _attn_kernel(qo, ko, mo, rb, gs, kr, q, k, v, km, vm, b, mb, o, *scr, chunk=lay.chunk)

    return pl.pallas_call(
        kern,
        grid_spec=grid_spec,
        out_shape=jax.ShapeDtypeStruct((rows, ATTN_WIDTH), BF16),
        input_output_aliases={13: 0},
        compiler_params=pltpu.CompilerParams(
            dimension_semantics=("arbitrary", "arbitrary"), vmem_limit_bytes=VMEM_LIMIT),
        name="attn_grid",
    )(*tabs, qkv, qkv, qkv, qkv, qkv, bias8, mbias, o_init)


def _attn_meta_kernel(moff_ref, x_ref, mb_ref, o_in_ref, o_ref):
    del moff_ref, o_in_ref
    x = x_ref[...].astype(F32)
    outs = []
    for h in range(ATTN_HEADS):
        q = x[:, h * HEAD_DIM:(h + 1) * HEAD_DIM].astype(BF16)
        k = x[:, ATTN_WIDTH + h * HEAD_DIM:ATTN_WIDTH + (h + 1) * HEAD_DIM].astype(BF16)
        v = x[:, 2 * ATTN_WIDTH + h * HEAD_DIM:2 * ATTN_WIDTH + (h + 1) * HEAD_DIM].astype(BF16)
        s = _dot_nt(q, k) + mb_ref[h:h + 1, :]
        m = jnp.max(s, axis=1, keepdims=True)
        p = jnp.exp(s - m)
        l = jnp.sum(p, axis=1, keepdims=True)
        outs.append(_dot(p.astype(BF16), v) / l)
    o_ref[...] = jnp.concatenate(outs, axis=1).astype(o_ref.dtype)


def _attention_meta(qkv, o_init, meta_bias, lay):
    rows = qkv.shape[0]
    n_seq = len(lay.seq_start)
    moff = jnp.asarray(np.asarray(lay.seq_start, np.int32))
    el = pl.Element
    grid_spec = pltpu.PrefetchScalarGridSpec(
        num_scalar_prefetch=1,
        grid=(n_seq,),
        in_specs=[
            pl.BlockSpec((el(N_META), el(3 * ATTN_WIDTH)),
                         lambda b, mo: (pl.multiple_of(mo[b], ROW_MULT), 0)),
            pl.BlockSpec((ATTN_HEADS, N_META), lambda b, mo: (0, 0)),
            pl.BlockSpec(memory_space=pl.ANY),
        ],
        out_specs=pl.BlockSpec((el(N_META), el(ATTN_WIDTH)),
                               lambda b, mo: (pl.multiple_of(mo[b], ROW_MULT), 0)),
    )
    return pl.pallas_call(
        _attn_meta_kernel,
        grid_spec=grid_spec,
        out_shape=jax.ShapeDtypeStruct((rows, ATTN_WIDTH), BF16),
        input_output_aliases={3: 0},
        compiler_params=pltpu.CompilerParams(dimension_semantics=("arbitrary",)),
        name="attn_meta",
    )(moff, qkv, meta_bias, o_init)


def _attn_bias_tables(rpb, meta_bias):
    cq = np.arange(GRID_W)
    cs = np.clip(cq - NA_COLS // 2, 0, GRID_W - NA_COLS)
    ck = np.arange(GRID_W)
    valid = (ck[None, :] >= cs[:, None]) & (ck[None, :] < cs[:, None] + NA_COLS)
    dc = ck[None, :] - cq[:, None] + NA_COLS - 1
    onehot = (np.arange(2 * NA_COLS - 1)[:, None, None] == dc[None]) & valid[None]
    t = jnp.einsum('hrd,dqk->hqrk', rpb.astype(F32), jnp.asarray(onehot, F32),
                   precision=lax.Precision.HIGHEST)
    t = t + jnp.asarray(np.where(valid, 0.0, NEG), F32)[None, :, None, :]
    bias8 = jnp.stack([t[:, :, v:v + NA_ROWS, :].reshape(ATTN_HEADS, GRID_W, NA_ROWS * GRID_W)
                       for v in range(NA_ROWS)], axis=1)
    lane = np.arange(LANES)
    head = np.arange(ATTN_HEADS)
    slot = (head % HEAD_PAIR) * N_META
    src = np.clip(lane[None, :] - slot[:, None], 0, N_META - 1)
    ok = (lane[None, :] >= slot[:, None]) & (lane[None, :] < slot[:, None] + N_META)
    mb = jnp.where(jnp.asarray(ok), jnp.take_along_axis(meta_bias.astype(F32), jnp.asarray(src), 1), NEG)
    return bias8, mb.reshape(ATTN_HEADS, 1, LANES)


def _mix_out_kernel(*refs, moe, n_valid):
    if moe:
        (oa_ref, u_ref, up_ref, un_ref, pos_ref, rem_ref, pw_ref, ps_ref, ga_ref, gp_ref, wo_ref,
         h_ref, lf_ref, rt_ref, tri_ref, h1_ref, hn_ref, route_ref, cnt_ref, ext_ref, hi_ref, lo_ref,
         run_ref) = refs
    else:
        (oa_ref, u_ref, up_ref, un_ref, pos_ref, rem_ref, pw_ref, ps_ref, ga_ref, gp_ref, wo_ref,
         h_ref, lf_ref, h1_ref, hn_ref, ext_ref, hi_ref, lo_ref) = refs
    tm = u_ref.shape[0]
    hal = POOL_HALO
    ext_ref[0:hal, :] = up_ref[...]
    ext_ref[hal:hal + tm, :] = u_ref[...]
    ext_ref[hal + tm:hal + tm + hal, :] = un_ref[...]
    ext = ext_ref[...]
    ext_hi = ext.astype(BF16)
    hi_ref[...] = ext_hi
    lo_ref[...] = (ext - ext_hi.astype(F32)).astype(BF16)
    band_rows = POOL_SUB + 2 * hal
    dist = (lax.broadcasted_iota(jnp.int32, (POOL_SUB, band_rows), 1)
            - lax.broadcasted_iota(jnp.int32, (POOL_SUB, band_rows), 0) - hal)

    for r0 in range(0, tm, MIX_SUB):
        rows = slice(r0, r0 + MIX_SUB)
        mixed = []
        for g, w in enumerate(POOL_WINDOWS):
            half = w // 2
            cols = slice(g * POOL_GROUP, (g + 1) * POOL_GROUP)
            slabs = []
            for p0 in range(r0, r0 + MIX_SUB, POOL_SUB):
                back = jnp.minimum(pos_ref[p0:p0 + POOL_SUB, :], half)
                fwd = jnp.minimum(rem_ref[p0:p0 + POOL_SUB, :], half)
                band = jnp.where(jnp.logical_and(dist >= -back, dist < fwd), 1.0, 0.0).astype(BF16)
                win = (_dot(band, hi_ref[p0:p0 + band_rows, cols])
                       + _dot(band, lo_ref[p0:p0 + band_rows, cols]))
                mean = win / (back + fwd).astype(F32)
                slabs.append(mean - ext_ref[hal + p0:hal + p0 + POOL_SUB, cols])
            pooled = jnp.concatenate(slabs, axis=0).astype(BF16)
            mixed.append(_dot(pooled, pw_ref[g]))
        o_pool = jnp.concatenate(mixed, axis=1) * ps_ref[...]
        n_pool = _rms(o_pool, gp_ref[...]).astype(BF16)
        n_attn = _rms(oa_ref[rows, :].astype(F32), ga_ref[...]).astype(BF16)
        mix = _dot(n_attn, wo_ref[0:ATTN_WIDTH, :]) + _dot(n_pool, wo_ref[ATTN_WIDTH:D_MODEL, :])
        h1 = h_ref[rows, :] + mix
        h1_ref[rows, :] = h1
        hn = _rms(h1, lf_ref[...])
        hn_ref[rows, :] = hn.astype(hn_ref.dtype)
    if not moe:
        return

    hn = hn_ref[...]
    i = pl.program_id(0)

    @pl.when(i == 0)
    def _():
        run_ref[...] = jnp.zeros_like(run_ref)

    hi = hn.astype(BF16)
    lo = (hn - hi.astype(F32)).astype(BF16)
    l1 = _dot_nt(rt_ref[...], hi)
    l2 = _dot_nt(rt_ref[0:16, :], lo)
    lg = l1[0:N_EXPERTS] + l1[16:16 + N_EXPERTS] + l2[0:N_EXPERTS]
    eidx = lax.broadcasted_iota(jnp.int32, (N_EXPERTS, tm), 0).astype(F32)
    none = float(N_EXPERTS)
    m1 = jnp.max(lg, axis=0, keepdims=True)
    i1 = jnp.min(jnp.where(lg == m1, eidx, none), axis=0, keepdims=True)
    sel1 = eidx == i1
    lg2 = jnp.where(sel1, -jnp.inf, lg)
    m2 = jnp.max(lg2, axis=0, keepdims=True)
    i2 = jnp.min(jnp.where(lg2 == m2, eidx, none), axis=0, keepdims=True)
    sel2 = eidx == i2
    t = jnp.exp(m2 - m1)
    g1 = 1.0 / (1.0 + t)
    g2 = t / (1.0 + t)
    rowid = i * tm + lax.broadcasted_iota(jnp.int32, (1, tm), 1)
    valid = rowid < n_valid
    c = jnp.where(jnp.logical_and(jnp.logical_or(sel1, sel2), valid), 1.0, 0.0)
    c16 = jnp.concatenate([c, jnp.zeros_like(c)], axis=0).astype(BF16)
    cs = _dot(c16, tri_ref[...])[0:N_EXPERTS]
    rank_all = run_ref[:, 0:1] + cs - 1.0
    rank1 = jnp.sum(jnp.where(sel1, rank_all, 0.0), axis=0, keepdims=True)
    rank2 = jnp.sum(jnp.where(sel2, rank_all, 0.0), axis=0, keepdims=True)
    run_new = run_ref[...] + jnp.sum(c, axis=1, keepdims=True)
    run_ref[...] = run_new
    cnt_ref[...] = run_new
    route = jnp.zeros((N_EXPERTS, tm), F32)
    for k, row in enumerate((i1, i2, g1, g2, rank1, rank2)):
        route = jnp.where(eidx == float(k), row, route)
    route_ref[...] = route


def _mix_out(o_attn, u, h, lay, pool_w_bf, pool_scale, g_attn, g_pool, w_out_bf, ln_ffn,
             router_t=None):
    rows = h.shape[0]
    tm = TM_MIX
    moe = router_t is not None
    blk8 = tm // POOL_HALO
    n8 = rows // POOL_HALO
    row1 = lambda a: a.reshape(1, -1)
    in_specs = [
        pl.BlockSpec((tm, ATTN_WIDTH), lambda i: (i, 0)),
        pl.BlockSpec((tm, POOL_WIDTH), lambda i: (i, 0)),
        pl.BlockSpec((POOL_HALO, POOL_WIDTH), lambda i: (jnp.maximum(i * blk8 - 1, 0), 0)),
        pl.BlockSpec((POOL_HALO, POOL_WIDTH), lambda i: (jnp.minimum((i + 1) * blk8, n8 - 1), 0)),
        pl.BlockSpec((tm, 1), lambda i: (i, 0)),
        pl.BlockSpec((tm, 1), lambda i: (i, 0)),
        pl.BlockSpec((len(POOL_WINDOWS), POOL_GROUP, POOL_GROUP), lambda i: (0, 0, 0)),
        pl.BlockSpec((1, POOL_WIDTH), lambda i: (0, 0)),
        pl.BlockSpec((1, ATTN_WIDTH), lambda i: (0, 0)),
        pl.BlockSpec((1, POOL_WIDTH), lambda i: (0, 0)),
        pl.BlockSpec((D_MODEL, D_MODEL), lambda i: (0, 0), pipeline_mode=pl.Buffered(1)),
        pl.BlockSpec((tm, D_MODEL), lambda i: (i, 0)),
        pl.BlockSpec((1, D_MODEL), lambda i: (0, 0)),
    ]
    args = [o_attn, u, u, u, jnp.asarray(lay.pos), jnp.asarray(lay.rem), pool_w_bf,
            row1(pool_scale), row1(g_attn), row1(g_pool), w_out_bf, h, row1(ln_ffn)]
    out_specs = [pl.BlockSpec((tm, D_MODEL), lambda i: (i, 0)),
                 pl.BlockSpec((tm, D_MODEL), lambda i: (i, 0))]
    out_shape = [jax.ShapeDtypeStruct((rows, D_MODEL), F32),
                 jax.ShapeDtypeStruct((rows, D_MODEL), F32 if moe else BF16)]
    scratch = [pltpu.VMEM((tm + 2 * POOL_HALO, POOL_WIDTH), F32),
               pltpu.VMEM((tm + 2 * POOL_HALO, POOL_WIDTH), BF16),
               pltpu.VMEM((tm + 2 * POOL_HALO, POOL_WIDTH), BF16)]
    if moe:
        tri = jnp.asarray(np.triu(np.ones((tm, tm), np.float32)), BF16)
        in_specs += [pl.BlockSpec((32, D_MODEL), lambda i: (0, 0)),
                     pl.BlockSpec((tm, tm), lambda i: (0, 0))]
        args += [router_t, tri]
        out_specs += [pl.BlockSpec((N_EXPERTS, tm), lambda i: (0, i)),
                      pl.BlockSpec((N_EXPERTS, LANES), lambda i: (0, 0))]
        out_shape += [jax.ShapeDtypeStruct((N_EXPERTS, rows), F32),
                      jax.ShapeDtypeStruct((N_EXPERTS, LANES), F32)]
        scratch += [pltpu.VMEM((N_EXPERTS, LANES), F32)]
    return pl.pallas_call(
        functools.partial(_mix_out_kernel, moe=moe, n_valid=lay.n_valid),
        grid=(rows // tm,),
        in_specs=in_specs,
        out_specs=out_specs,
        out_shape=out_shape,
        scratch_shapes=scratch,
        compiler_params=pltpu.CompilerParams(
            dimension_semantics=("arbitrary",), vmem_limit_bytes=VMEM_LIMIT),
        name="mix_out_moe" if moe else "mix_out",
    )(*args)


def _swiglu_step(x, wg_ref, wu_ref, wd_ref):
    gate = _dot(x, wg_ref[...])
    up = _dot(x, wu_ref[...])
    mid = (gate / (1.0 + jnp.exp(-gate))) * up
    return _dot(mid.astype(BF16), wd_ref[...])


def _ffn_dense_kernel(x_ref, wg_ref, wu_ref, wd_ref, res_ref, o_ref):
    @pl.when(pl.program_id(1) == 0)
    def _():
        o_ref[...] = res_ref[...]

    o_ref[...] += _swiglu_step(x_ref[...], wg_ref, wu_ref, wd_ref)


def _ffn_dense(x, wg_bf, wu_bf, wd_bf, res):
    rows = x.shape[0]
    tm, tf = TM_FFN, TF_FFN
    return pl.pallas_call(
        _ffn_dense_kernel,
        grid=(rows // tm, D_FF // tf),
        in_specs=[
            pl.BlockSpec((tm, D_MODEL), lambda i, j: (i, 0)),
            pl.BlockSpec((D_MODEL, tf), lambda i, j: (0, j)),
            pl.BlockSpec((D_MODEL, tf), lambda i, j: (0, j)),
            pl.BlockSpec((tf, D_MODEL), lambda i, j: (j, 0)),
            pl.BlockSpec((tm, D_MODEL), lambda i, j: (i, 0)),
        ],
        out_specs=pl.BlockSpec((tm, D_MODEL), lambda i, j: (i, 0)),
        out_shape=jax.ShapeDtypeStruct((rows, D_MODEL), F32),
        compiler_params=pltpu.CompilerParams(
            dimension_semantics=("arbitrary", "arbitrary"), vmem_limit_bytes=VMEM_LIMIT),
        name="ffn_dense",
    )(x, wg_bf, wu_bf, wd_bf, res)


def _ffn_moe_kernel(te_ref, nu_ref, src_ref, hn_ref, wg_ref, wu_ref, wd_ref, o_ref,
                    xbuf_ref, xb_ref, sem):
    del te_ref
    tm = o_ref.shape[0]
    n_tiles = pl.num_programs(0)
    nf = pl.num_programs(1)
    share = xbuf_ref.shape[1] // D_FF_STEPS
    i = pl.program_id(0)
    j = pl.program_id(1)
    n_used = nu_ref[0]
    used = i < n_used
    slot = lax.rem(i, 2)

    def row_copy(tile, row, buf):
        return pltpu.make_async_copy(hn_ref.at[pl.ds(src_ref[tile * tm + row], 1)],
                                     xbuf_ref.at[buf, pl.ds(row, 1)], sem.at[buf])

    def tile_rows(buf):
        return pltpu.make_async_copy(hn_ref.at[pl.ds(0, xbuf_ref.shape[1])], xbuf_ref.at[buf],
                                     sem.at[buf])

    @pl.when(j == 0)
    def _():
        @pl.when(i == 0)
        def _():
            def body(r, c):
                row_copy(0, r, 0).start()
                return c

            lax.fori_loop(0, xbuf_ref.shape[1], body, 0, unroll=8)

        @pl.when(i <= n_used)
        def _():
            tile_rows(slot).wait()

        @pl.when(used)
        def _():
            xb_ref[...] = xbuf_ref[slot, 0:tm, :].astype(BF16)

        o_ref[...] = jnp.zeros_like(o_ref)

    @pl.when(used)
    def _():
        for k in range(share):
            row_copy(i + 1, j * share + k, 1 - slot).start()
        o_ref[...] += _swiglu_step(xb_ref[...], wg_ref, wu_ref, wd_ref)

    @pl.when(jnp.logical_and(used, jnp.logical_and(i == n_tiles - 1, j == nf - 1)))
    def _():
        tile_rows(1 - slot).wait()


def _ffn_moe(hn, src, wg_bf, wu_bf, wd_bf, tile_expert, n_used, lay):
    tm, tf = TM_FFN, TF_FFN
    nf = D_FF // tf

    def jj(i, j, nu):
        return jnp.where(i < nu[0], j, nf - 1)

    grid_spec = pltpu.PrefetchScalarGridSpec(
        num_scalar_prefetch=3,
        grid=(lay.slots // tm, nf),
        in_specs=[
            pl.BlockSpec(memory_space=pl.ANY),
            pl.BlockSpec((None, D_MODEL, tf), lambda i, j, te, nu, sr: (te[i], 0, jj(i, j, nu))),
            pl.BlockSpec((None, D_MODEL, tf), lambda i, j, te, nu, sr: (te[i], 0, jj(i, j, nu))),
            pl.BlockSpec((None, tf, D_MODEL), lambda i, j, te, nu, sr: (te[i], jj(i, j, nu), 0)),
        ],
        out_specs=pl.BlockSpec((tm, D_MODEL), lambda i, j, te, nu, sr: (i, 0)),
        scratch_shapes=[
            pltpu.VMEM((2, GATHER_ROWS, D_MODEL), F32),
            pltpu.VMEM((tm, D_MODEL), BF16),
            pltpu.SemaphoreType.DMA((2,)),
        ],
    )
    return pl.pallas_call(
        _ffn_moe_kernel,
        grid_spec=grid_spec,
        out_shape=jax.ShapeDtypeStruct((lay.slots, D_MODEL), F32),
        compiler_params=pltpu.CompilerParams(
            dimension_semantics=("arbitrary", "arbitrary"), vmem_limit_bytes=VMEM_LIMIT),
        name="ffn_moe",
    )(tile_expert, n_used, src, hn, wg_bf, wu_bf, wd_bf)


def _slot_src_kernel(pos_ref, src_ref, *, td, slots):
    i = pl.program_id(0)

    @pl.when(i == 0)
    def _():
        def init(s, c):
            src_ref[s] = 0
            return c

        lax.fori_loop(0, slots, init, 0, unroll=8)

    base = i * td

    def body(r, c):
        src_ref[pos_ref[0, r]] = base + r
        src_ref[pos_ref[1, r]] = base + r
        return c

    lax.fori_loop(0, td, body, 0, unroll=8)


def _slot_sources(pos3, lay):
    td = lay.td
    return pl.pallas_call(
        functools.partial(_slot_src_kernel, td=td, slots=lay.src_len),
        grid=(lay.n_valid // td,),
        in_specs=[pl.BlockSpec((None, 2, td), lambda i: (i, 0, 0), memory_space=pltpu.SMEM)],
        out_specs=pl.BlockSpec(memory_space=pltpu.SMEM),
        out_shape=jax.ShapeDtypeStruct((lay.src_len,), jnp.int32),
        compiler_params=pltpu.CompilerParams(dimension_semantics=("arbitrary",)),
        name="slot_sources",
    )(pos3)


def _combine_kernel(pos_ref, gate_ref, h_ref, gf_ref, y_ref, o_ref, buf_ref, sem, *, tc):
    def issue(r, c):
        for k in range(2):
            pltpu.make_async_copy(y_ref.at[pl.ds(pos_ref[k, r], 1)],
                                  buf_ref.at[k, pl.ds(r, 1)], sem).start()
        return c

    lax.fori_loop(0, tc, issue, 0, unroll=8)
    for k in range(2):
        pltpu.make_async_copy(y_ref.at[pl.ds(0, tc)], buf_ref.at[k], sem).wait()
    gate = gate_ref[...]
    y = gate[:, 0:1] * buf_ref[0] + gate[:, 1:2] * buf_ref[1]
    o_ref[...] = _rms(h_ref[...] + y, gf_ref[...]).astype(o_ref.dtype)


def _combine(h1, y_sorted, pos, gates, g_final, row0, batch, tokens):
    tc = TM_OUT
    per = tokens // tc
    el = pl.Element
    seq = N_META + tokens

    def grid_rows(a):
        a = a[:, row0:row0 + batch * seq].reshape(a.shape[0], batch, seq)[:, :, N_META:]
        return a.reshape(a.shape[0], batch * tokens)

    pos_g = grid_rows(pos).reshape(2, batch * per, tc).transpose(1, 0, 2)
    gates_g = grid_rows(gates).T

    def rmap(t):
        return (pl.multiple_of(row0 + N_META + (t // per) * seq + (t % per) * tc, ROW_MULT), 0)

    return pl.pallas_call(
        functools.partial(_combine_kernel, tc=tc),
        grid=(batch * per,),
        in_specs=[
            pl.BlockSpec((None, 2, tc), lambda t: (t, 0, 0), memory_space=pltpu.SMEM),
            pl.BlockSpec((tc, 2), lambda t: (t, 0)),
            pl.BlockSpec((el(tc), el(D_MODEL)), rmap),
            pl.BlockSpec((1, D_MODEL), lambda t: (0, 0)),
            pl.BlockSpec(memory_space=pl.ANY),
        ],
        out_specs=pl.BlockSpec((None, tc, D_MODEL), lambda t: (t // per, t % per, 0)),
        out_shape=jax.ShapeDtypeStruct((batch, tokens, D_MODEL), F32),
        scratch_shapes=[pltpu.VMEM((2, tc, D_MODEL), F32), pltpu.SemaphoreType.DMA(())],
        compiler_params=pltpu.CompilerParams(
            dimension_semantics=("arbitrary",), vmem_limit_bytes=VMEM_LIMIT),
        name="combine",
    )(pos_g, gates_g, h1, g_final.reshape(1, D_MODEL), y_sorted)


def _routing_tables(route, cnt, lay):
    tmx = TM_FFN
    counts = cnt[:, 0].astype(jnp.int32)
    padded = (counts + tmx - 1) // tmx * tmx
    ends = jnp.cumsum(padded)
    offs = ends - padded
    e_idx = route[0:2].astype(jnp.int32)
    rank = route[4:6].astype(jnp.int32)
    onehot = e_idx[:, :, None] == jnp.arange(N_EXPERTS, dtype=jnp.int32)
    pos = jnp.sum(jnp.where(onehot, offs, 0), axis=-1) + rank
    pos = jnp.where(jnp.arange(pos.shape[1])[None, :] < lay.n_valid, pos, 0)
    n_tiles = lay.slots // tmx
    tile_start = jnp.arange(n_tiles, dtype=jnp.int32) * tmx
    tile_expert = jnp.minimum(jnp.sum(tile_start[:, None] >= ends[None, :], axis=1),
                              N_EXPERTS - 1).astype(jnp.int32)
    n_used = (ends[-1] // tmx).reshape(1).astype(jnp.int32)
    return pos, route[2:4], tile_expert, n_used


def _forward(xs, meta_tokens, ln_mix, w_in, rpb, meta_bias, pool_w, pool_scale, g_attn_out,
             g_pool_out, w_out, ln_ffn, ffn_w_gate, ffn_w_up, ffn_w_down, router, moe_w_gate,
             moe_w_up, moe_w_down, g_final):
    groups = [(x.shape[0], x.shape[1]) for x in xs]
    lay = _Layout(groups)
    depth = ln_mix.shape[0]
    assert depth == 2

    parts = []
    for x in xs:
        for b in range(x.shape[0]):
            parts += [meta_tokens.astype(F32), x[b]]
    parts.append(jnp.zeros((lay.rows - lay.n_valid, D_MODEL), F32))
    h = jnp.concatenate(parts, axis=0)

    outs = None
    for layer in range(depth):
        qkv, u = _in_proj(h, ln_mix[layer], w_in[layer].astype(BF16))
        bias8, mbias = _attn_bias_tables(rpb[layer], meta_bias[layer])
        o_attn = jnp.zeros((lay.rows, ATTN_WIDTH), BF16)
        o_attn = _attention_grid(qkv, o_attn, bias8, mbias, lay)
        o_attn = _attention_meta(qkv, o_attn, meta_bias[layer].astype(F32), lay)
        mix_args = (o_attn, u, h, lay, pool_w[layer].astype(BF16), pool_scale[layer],
                    g_attn_out[layer], g_pool_out[layer], w_out[layer].astype(BF16), ln_ffn[layer])
        i = layer // 2
        if layer % 2 == 0:
            h1, hn = _mix_out(*mix_args)
            h = _ffn_dense(hn, ffn_w_gate[i].astype(BF16), ffn_w_up[i].astype(BF16),
                           ffn_w_down[i].astype(BF16), h1)
        else:
            rt = router[i].T
            rt_hi = rt.astype(BF16)
            rt_lo = (rt - rt_hi.astype(F32)).astype(BF16)
            zpad = jnp.zeros_like(rt_hi)
            router_t = jnp.concatenate([rt_hi, zpad, rt_lo, zpad], axis=0)
            h1, hn, route, cnt = _mix_out(*mix_args, router_t=router_t)
            pos, gates, tile_expert, n_used = _routing_tables(route, cnt, lay)
            td = lay.td
            pos3 = pos[:, :lay.n_valid].reshape(2, lay.n_valid // td, td).transpose(1, 0, 2)
            src = _slot_sources(pos3, lay)
            y_sorted = _ffn_moe(hn, src, moe_w_gate[i].astype(BF16), moe_w_up[i].astype(BF16),
                                moe_w_down[i].astype(BF16), tile_expert, n_used, lay)
            outs = []
            seq = 0
            for b, t in groups:
                outs.append(_combine(h1, y_sorted, pos, gates, g_final, lay.seq_start[seq], b, t))
                seq += b
    return tuple(outs)


def kernel(x_prompt, x_sample, meta_tokens, ln_mix, w_in, rpb, meta_bias, pool_w, pool_scale,
           g_attn_out, g_pool_out, w_out, ln_ffn, ffn_w_gate, ffn_w_up, ffn_w_down, router,
           moe_w_gate, moe_w_up, moe_w_down, g_final):
    return _forward((x_prompt, x_sample), meta_tokens, ln_mix, w_in, rpb, meta_bias, pool_w,
                    pool_scale, g_attn_out, g_pool_out, w_out, ln_ffn, ffn_w_gate, ffn_w_up,
                    ffn_w_down, router, moe_w_gate, moe_w_up, moe_w_down, g_final)
```

```python
import functools

import numpy as np
import jax
import jax.numpy as jnp
from jax import lax
from jax.experimental import pallas as pl
from jax.experimental.pallas import tpu as pltpu

F32 = jnp.float32
BF16 = jnp.bfloat16

D_MODEL = 2048
GRID_W = 64
N_META = 16
ATTN_HEADS = 16
HEAD_DIM = 64
ATTN_WIDTH = ATTN_HEADS * HEAD_DIM
POOL_WIDTH = D_MODEL - ATTN_WIDTH
POOL_WINDOWS = (2, 4, 8, 16)
POOL_GROUP = POOL_WIDTH // len(POOL_WINDOWS)
POOL_HALO = max(POOL_WINDOWS) // 2
NA_ROWS = 8
NA_COLS = 16
D_FF = 5632
N_EXPERTS = 8
EPS = 1e-6
NEG = -1e30

LANES = 128
HEAD_PAIR = LANES // HEAD_DIM
VMEM_LIMIT = 56 * 1024 * 1024

TM_IN = 1024
TN_IN = 1024
TM_MIX = 512
MIX_SUB = 512
POOL_SUB = 128
TM_FFN = 512
TF_FFN = 512
D_FF_STEPS = D_FF // TF_FFN
GATHER_SHARE = (-(-TM_FFN // D_FF_STEPS) + 7) // 8 * 8
GATHER_ROWS = GATHER_SHARE * D_FF_STEPS
TM_OUT = 256
ATTN_CHUNK = 32
ATTN_BLOCK = 16
ROW_ALIGN = 1024
ROW_MULT = 16


def _round_up(x, m):
    return (x + m - 1) // m * m


def _rms(x, g):
    return x * lax.rsqrt(jnp.mean(x * x, axis=-1, keepdims=True) + EPS) * g


def _dot(a, b):
    return jnp.dot(a, b, preferred_element_type=F32)


def _dot_nt(a, b):
    return lax.dot_general(a, b, (((1,), (1,)), ((), ())), preferred_element_type=F32)


class _Layout:
    def __init__(self, groups):
        self.groups = groups
        self.seq_start, self.seq_tokens = [], []
        row = 0
        for b, t in groups:
            assert t % (GRID_W * NA_ROWS) == 0
            for _ in range(b):
                self.seq_start.append(row)
                self.seq_tokens.append(t)
                row += N_META + t
        self.n_valid = row
        self.rows = _round_up(row, ROW_ALIGN)
        pos = np.zeros((self.rows, 1), np.int32)
        rem = np.ones((self.rows, 1), np.int32)
        for s, t in zip(self.seq_start, self.seq_tokens):
            n = N_META + t
            pos[s:s + n, 0] = np.arange(n)
            rem[s:s + n, 0] = n - np.arange(n)
        self.pos, self.rem = pos, rem
        items = []
        self.chunk = ATTN_CHUNK
        self.kwin = self.chunk + NA_ROWS
        for s, t in zip(self.seq_start, self.seq_tokens):
            g = t // GRID_W
            assert g % self.chunk == 0
            for c in range(g // self.chunk):
                r_base = c * self.chunk
                kv_row0 = int(np.clip(r_base - NA_ROWS // 2, 0, max(g - self.kwin, 0)))
                items.append((s + N_META + r_base * GRID_W, s + N_META + kv_row0 * GRID_W,
                              s, r_base, g, kv_row0))
        self.items = np.asarray(items, np.int32)
        assert int(self.items[:, 1].max()) + self.kwin * GRID_W <= self.rows
        self.td = max(d for d in range(8, 513, 8) if self.n_valid % d == 0)
        self.slots = _round_up(2 * self.n_valid + N_EXPERTS * (TM_FFN - 1), TM_FFN)
        self.src_len = self.slots + GATHER_ROWS


def _in_proj_kernel(x_ref, g_ref, w_ref, qkv_ref, u_ref, xn_ref):
    j = pl.program_id(1)

    @pl.when(j == 0)
    def _():
        xn_ref[...] = _rms(x_ref[...], g_ref[...]).astype(BF16)

    z = _dot(xn_ref[...], w_ref[...])

    @pl.when(j == 0)
    def _():
        qkv_ref[...] = (z * (HEAD_DIM ** -0.5)).astype(BF16)

    @pl.when(jnp.logical_and(j > 0, j < 3))
    def _():
        qkv_ref[...] = z.astype(BF16)

    @pl.when(j == 3)
    def _():
        u_ref[...] = z


def _in_proj(h, g, w_bf):
    rows = h.shape[0]
    assert TN_IN == ATTN_WIDTH == POOL_WIDTH
    return pl.pallas_call(
        _in_proj_kernel,
        grid=(rows // TM_IN, 4),
        in_specs=[
            pl.BlockSpec((TM_IN, D_MODEL), lambda i, j: (i, 0)),
            pl.BlockSpec((1, D_MODEL), lambda i, j: (0, 0)),
            pl.BlockSpec((D_MODEL, TN_IN), lambda i, j: (0, j)),
        ],
        out_specs=[
            pl.BlockSpec((TM_IN, TN_IN), lambda i, j: (i, jnp.minimum(j, 2))),
            pl.BlockSpec((TM_IN, TN_IN), lambda i, j: (i, 0)),
        ],
        out_shape=[
            jax.ShapeDtypeStruct((rows, 3 * ATTN_WIDTH), BF16),
            jax.ShapeDtypeStruct((rows, POOL_WIDTH), F32),
        ],
        scratch_shapes=[pltpu.VMEM((TM_IN, D_MODEL), BF16)],
        compiler_params=pltpu.CompilerParams(
            dimension_semantics=("arbitrary", "arbitrary"), vmem_limit_bytes=VMEM_LIMIT),
        name="in_proj",
    )(h, g.reshape(1, D_MODEL), w_bf)


def _attn_kernel(qoff_ref, kvoff_ref, moff_ref, rbase_ref, gsz_ref, kvrow_ref,
                 q_ref, k_ref, v_ref, km_ref, vm_ref, bias_ref, mb_ref, o_ref,
                 ka_ref, kb_ref, kme_ref, vme_ref, s_ref, p_ref, l_ref, m_ref, *, chunk):
    it = pl.program_id(1)
    r_base = rbase_ref[it]
    g_rows = gsz_ref[it]
    kv_row0 = kvrow_ref[it]
    is_a = lax.broadcasted_iota(jnp.int32, (1, LANES), 1) < HEAD_DIM

    k = k_ref[...]
    kz = jnp.zeros_like(k)
    ka_ref[...] = jnp.where(is_a, k, kz)
    kb_ref[...] = jnp.where(is_a, kz, k)
    km = km_ref[...]
    kmz = jnp.zeros_like(km)
    kme_ref[...] = jnp.zeros_like(kme_ref)
    kme_ref[0:N_META, :] = jnp.where(is_a, km, kmz)
    kme_ref[N_META:2 * N_META, :] = jnp.where(is_a, kmz, km)
    vme_ref[...] = jnp.zeros_like(vme_ref)
    vme_ref[0:N_META, :] = vm_ref[...]
    vme_ref[N_META:2 * N_META, :] = vm_ref[...]

    nk = NA_ROWS * GRID_W
    krefs = (ka_ref, kb_ref)

    def row_window(r):
        rg = r_base + r
        rs = jnp.clip(rg - NA_ROWS // 2, 0, g_rows - NA_ROWS)
        var = rs - rg + (NA_ROWS - 1)
        kl = pl.multiple_of((rs - kv_row0) * GRID_W, GRID_W)
        q0 = pl.multiple_of(r * GRID_W, GRID_W)
        return var, kl, q0

    def block(bi, carry):
        for rr in range(ATTN_BLOCK):
            var, kl, q0 = row_window(bi * ATTN_BLOCK + rr)
            rows = slice(rr * GRID_W, (rr + 1) * GRID_W)
            q = q_ref[pl.ds(q0, GRID_W), :]
            s_meta = _dot_nt(q, kme_ref[...])
            for hd in range(HEAD_PAIR):
                sc = _dot_nt(q, krefs[hd][pl.ds(kl, nk), :]) + bias_ref[hd, var]
                sm = s_meta + mb_ref[hd]
                s_ref[hd, rows, 0:nk] = sc
                s_ref[hd, rows, nk:nk + LANES] = sm
                for c in range(nk // LANES):
                    sm = jnp.maximum(sm, sc[:, c * LANES:(c + 1) * LANES])
                m_ref[hd, rows, :] = sm
        for hd in range(HEAD_PAIR):
            for rr in range(ATTN_BLOCK):
                rows = slice(rr * GRID_W, (rr + 1) * GRID_W)
                m = jnp.max(m_ref[hd, rows, :], axis=1, keepdims=True)
                acc = None
                for c in range(nk // LANES + 1):
                    cols = slice(c * LANES, (c + 1) * LANES)
                    p = jnp.exp(s_ref[hd, rows, cols] - m)
                    acc = p if acc is None else acc + p
                    p_ref[hd, rows, cols] = p.astype(BF16)
                inv = 1.0 / jnp.sum(acc, axis=1, keepdims=True)
                l_ref[hd, rows, :] = jnp.broadcast_to(inv, (GRID_W, LANES))
        for rr in range(ATTN_BLOCK):
            var, kl, q0 = row_window(bi * ATTN_BLOCK + rr)
            rows = slice(rr * GRID_W, (rr + 1) * GRID_W)
            vv = v_ref[pl.ds(kl, nk), :]
            outs = []
            for hd in range(HEAD_PAIR):
                o = (_dot(p_ref[hd, rows, 0:nk], vv)
                     + _dot(p_ref[hd, rows, nk:nk + LANES], vme_ref[...]))
                outs.append(o * l_ref[hd, rows, :])
            o_ref[pl.ds(q0, GRID_W), :] = jnp.where(is_a, outs[0], outs[1]).astype(o_ref.dtype)
        return carry

    lax.fori_loop(0, chunk // ATTN_BLOCK, block, 0)


def _attention_grid(qkv, o_init, bias8, mbias, lay):
    rows = qkv.shape[0]
    qrows = lay.chunk * GRID_W
    krows = lay.kwin * GRID_W
    n_items = lay.items.shape[0]
    n_pairs = ATTN_HEADS // HEAD_PAIR
    tabs = [jnp.asarray(lay.items[:, c]) for c in range(6)]
    el = pl.Element

    def qmap(hp, it, qo, ko, mo, rb, gs, kr):
        return (pl.multiple_of(qo[it], ROW_MULT), pl.multiple_of(hp * LANES, LANES))

    def kmap(col0):
        def f(hp, it, qo, ko, mo, rb, gs, kr):
            return (pl.multiple_of(ko[it], ROW_MULT), pl.multiple_of(col0 + hp * LANES, LANES))
        return f

    def mmap(col0):
        def f(hp, it, qo, ko, mo, rb, gs, kr):
            return (pl.multiple_of(mo[it], ROW_MULT), pl.multiple_of(col0 + hp * LANES, LANES))
        return f

    grid_spec = pltpu.PrefetchScalarGridSpec(
        num_scalar_prefetch=6,
        grid=(n_pairs, n_items),
        in_specs=[
            pl.BlockSpec((el(qrows), el(LANES)), qmap),
            pl.BlockSpec((el(krows), el(LANES)), kmap(ATTN_WIDTH)),
            pl.BlockSpec((el(krows), el(LANES)), kmap(2 * ATTN_WIDTH)),
            pl.BlockSpec((el(N_META), el(LANES)), mmap(ATTN_WIDTH)),
            pl.BlockSpec((el(N_META), el(LANES)), mmap(2 * ATTN_WIDTH)),
            pl.BlockSpec((HEAD_PAIR, NA_ROWS, GRID_W, NA_ROWS * GRID_W),
                         lambda hp, it, *_: (hp, 0, 0, 0)),
            pl.BlockSpec((HEAD_PAIR, 1, LANES), lambda hp, it, *_: (hp, 0, 0)),
            pl.BlockSpec(memory_space=pl.ANY),
        ],
        out_specs=pl.BlockSpec((el(qrows), el(LANES)), qmap),
        scratch_shapes=[
            pltpu.VMEM((krows, LANES), BF16),
            pltpu.VMEM((krows, LANES), BF16),
            pltpu.VMEM((LANES, LANES), BF16),
            pltpu.VMEM((LANES, LANES), BF16),
            pltpu.VMEM((HEAD_PAIR, ATTN_BLOCK * GRID_W, NA_ROWS * GRID_W + LANES), F32),
            pltpu.VMEM((HEAD_PAIR, ATTN_BLOCK * GRID_W, NA_ROWS * GRID_W + LANES), BF16),
            pltpu.VMEM((HEAD_PAIR, ATTN_BLOCK * GRID_W, LANES), F32),
            pltpu.VMEM((HEAD_PAIR, ATTN_BLOCK * GRID_W, LANES), F32),
        ],
    )

    def kern(qo, ko, mo, rb, gs, kr, q, k, v, km, vm, b, mb, o_in, o, *scr):
        del o_in
        _attn_kernel(qo, ko, mo, rb, gs, kr, q, k, v, km, vm, b, mb, o, *scr, chunk=lay.chunk)

    return pl.pallas_call(
        kern,
        grid_spec=grid_spec,
        out_shape=jax.ShapeDtypeStruct((rows, ATTN_WIDTH), BF16),
        input_output_aliases={13: 0},
        compiler_params=pltpu.CompilerParams(
            dimension_semantics=("arbitrary", "arbitrary"), vmem_limit_bytes=VMEM_LIMIT),
        name="attn_grid",
    )(*tabs, qkv, qkv, qkv, qkv, qkv, bias8, mbias, o_init)


def _attn_meta_kernel(moff_ref, x_ref, mb_ref, o_in_ref, o_ref):
    del moff_ref, o_in_ref
    x = x_ref[...].astype(F32)
    outs = []
    for h in range(ATTN_HEADS):
        q = x[:, h * HEAD_DIM:(h + 1) * HEAD_DIM].astype(BF16)
        k = x[:, ATTN_WIDTH + h * HEAD_DIM:ATTN_WIDTH + (h + 1) * HEAD_DIM].astype(BF16)
        v = x[:, 2 * ATTN_WIDTH + h * HEAD_DIM:2 * ATTN_WIDTH + (h + 1) * HEAD_DIM].astype(BF16)
        s = _dot_nt(q, k) + mb_ref[h:h + 1, :]
        m = jnp.max(s, axis=1, keepdims=True)
        p = jnp.exp(s - m)
        l = jnp.sum(p, axis=1, keepdims=True)
        outs.append(_dot(p.astype(BF16), v) / l)
    o_ref[...] = jnp.concatenate(outs, axis=1).astype(o_ref.dtype)


def _attention_meta(qkv, o_init, meta_bias, lay):
    rows = qkv.shape[0]
    n_seq = len(lay.seq_start)
    moff = jnp.asarray(np.asarray(lay.seq_start, np.int32))
    el = pl.Element
    grid_spec = pltpu.PrefetchScalarGridSpec(
        num_scalar_prefetch=1,
        grid=(n_seq,),
        in_specs=[
            pl.BlockSpec((el(N_META), el(3 * ATTN_WIDTH)),
                         lambda b, mo: (pl.multiple_of(mo[b], ROW_MULT), 0)),
            pl.BlockSpec((ATTN_HEADS, N_META), lambda b, mo: (0, 0)),
            pl.BlockSpec(memory_space=pl.ANY),
        ],
        out_specs=pl.BlockSpec((el(N_META), el(ATTN_WIDTH)),
                               lambda b, mo: (pl.multiple_of(mo[b], ROW_MULT), 0)),
    )
    return pl.pallas_call(
        _attn_meta_kernel,
        grid_spec=grid_spec,
        out_shape=jax.ShapeDtypeStruct((rows, ATTN_WIDTH), BF16),
        input_output_aliases={3: 0},
        compiler_params=pltpu.CompilerParams(dimension_semantics=("arbitrary",)),
        name="attn_meta",
    )(moff, qkv, meta_bias, o_init)


def _attn_bias_tables(rpb, meta_bias):
    cq = np.arange(GRID_W)
    cs = np.clip(cq - NA_COLS // 2, 0, GRID_W - NA_COLS)
    ck = np.arange(GRID_W)
    valid = (ck[None, :] >= cs[:, None]) & (ck[None, :] < cs[:, None] + NA_COLS)
    dc = ck[None, :] - cq[:, None] + NA_COLS - 1
    onehot = (np.arange(2 * NA_COLS - 1)[:, None, None] == dc[None]) & valid[None]
    t = jnp.einsum('hrd,dqk->hqrk', rpb.astype(F32), jnp.asarray(onehot, F32),
                   precision=lax.Precision.HIGHEST)
    t = t + jnp.asarray(np.where(valid, 0.0, NEG), F32)[None, :, None, :]
    bias8 = jnp.stack([t[:, :, v:v + NA_ROWS, :].reshape(ATTN_HEADS, GRID_W, NA_ROWS * GRID_W)
                       for v in range(NA_ROWS)], axis=1)
    lane = np.arange(LANES)
    head = np.arange(ATTN_HEADS)
    slot = (head % HEAD_PAIR) * N_META
    src = np.clip(lane[None, :] - slot[:, None], 0, N_META - 1)
    ok = (lane[None, :] >= slot[:, None]) & (lane[None, :] < slot[:, None] + N_META)
    mb = jnp.where(jnp.asarray(ok), jnp.take_along_axis(meta_bias.astype(F32), jnp.asarray(src), 1), NEG)
    return bias8, mb.reshape(ATTN_HEADS, 1, LANES)


def _mix_out_kernel(*refs, moe, n_valid):
    if moe:
        (oa_ref, u_ref, up_ref, un_ref, pos_ref, rem_ref, pw_ref, ps_ref, ga_ref, gp_ref, wo_ref,
         h_ref, lf_ref, rt_ref, tri_ref, h1_ref, hn_ref, route_ref, cnt_ref, ext_ref, hi_ref, lo_ref,
         run_ref) = refs
    else:
        (oa_ref, u_ref, up_ref, un_ref, pos_ref, rem_ref, pw_ref, ps_ref, ga_ref, gp_ref, wo_ref,
         h_ref, lf_ref, h1_ref, hn_ref, ext_ref, hi_ref, lo_ref) = refs
    tm = u_ref.shape[0]
    hal = POOL_HALO
    ext_ref[0:hal, :] = up_ref[...]
    ext_ref[hal:hal + tm, :] = u_ref[...]
    ext_ref[hal + tm:hal + tm + hal, :] = un_ref[...]
    ext = ext_ref[...]
    ext_hi = ext.astype(BF16)
    hi_ref[...] = ext_hi
    lo_ref[...] = (ext - ext_hi.astype(F32)).astype(BF16)
    band_rows = POOL_SUB + 2 * hal
    dist = (lax.broadcasted_iota(jnp.int32, (POOL_SUB, band_rows), 1)
            - lax.broadcasted_iota(jnp.int32, (POOL_SUB, band_rows), 0) - hal)

    for r0 in range(0, tm, MIX_SUB):
        rows = slice(r0, r0 + MIX_SUB)
        mixed = []
        for g, w in enumerate(POOL_WINDOWS):
            half = w // 2
            cols = slice(g * POOL_GROUP, (g + 1) * POOL_GROUP)
            slabs = []
            for p0 in range(r0, r0 + MIX_SUB, POOL_SUB):
                back = jnp.minimum(pos_ref[p0:p0 + POOL_SUB, :], half)
                fwd = jnp.minimum(rem_ref[p0:p0 + POOL_SUB, :], half)
                band = jnp.where(jnp.logical_and(dist >= -back, dist < fwd), 1.0, 0.0).astype(BF16)
                win = (_dot(band, hi_ref[p0:p0 + band_rows, cols])
                       + _dot(band, lo_ref[p0:p0 + band_rows, cols]))
                mean = win / (back + fwd).astype(F32)
                slabs.append(mean - ext_ref[hal + p0:hal + p0 + POOL_SUB, cols])
            pooled = jnp.concatenate(slabs, axis=0).astype(BF16)
            mixed.append(_dot(pooled, pw_ref[g]))
        o_pool = jnp.concatenate(mixed, axis=1) * ps_ref[...]
        n_pool = _rms(o_pool, gp_ref[...]).astype(BF16)
        n_attn = _rms(oa_ref[rows, :].astype(F32), ga_ref[...]).astype(BF16)
        mix = _dot(n_attn, wo_ref[0:ATTN_WIDTH, :]) + _dot(n_pool, wo_ref[ATTN_WIDTH:D_MODEL, :])
        h1 = h_ref[rows, :] + mix
        h1_ref[rows, :] = h1
        hn = _rms(h1, lf_ref[...])
        hn_ref[rows, :] = hn.astype(hn_ref.dtype)
    if not moe:
        return

    hn = hn_ref[...]
    i = pl.program_id(0)

    @pl.when(i == 0)
    def _():
        run_ref[...] = jnp.zeros_like(run_ref)

    hi = hn.astype(BF16)
    lo = (hn - hi.astype(F32)).astype(BF16)
    l1 = _dot_nt(rt_ref[...], hi)
    l2 = _dot_nt(rt_ref[0:16, :], lo)
    lg = l1[0:N_EXPERTS] + l1[16:16 + N_EXPERTS] + l2[0:N_EXPERTS]
    eidx = lax.broadcasted_iota(jnp.int32, (N_EXPERTS, tm), 0).astype(F32)
    none = float(N_EXPERTS)
    m1 = jnp.max(lg, axis=0, keepdims=True)
    i1 = jnp.min(jnp.where(lg == m1, eidx, none), axis=0, keepdims=True)
    sel1 = eidx == i1
    lg2 = jnp.where(sel1, -jnp.inf, lg)
    m2 = jnp.max(lg2, axis=0, keepdims=True)
    i2 = jnp.min(jnp.where(lg2 == m2, eidx, none), axis=0, keepdims=True)
    sel2 = eidx == i2
    t = jnp.exp(m2 - m1)
    g1 = 1.0 / (1.0 + t)
    g2 = t / (1.0 + t)
    rowid = i * tm + lax.broadcasted_iota(jnp.int32, (1, tm), 1)
    valid = rowid < n_valid
    c = jnp.where(jnp.logical_and(jnp.logical_or(sel1, sel2), valid), 1.0, 0.0)
    c16 = jnp.concatenate([c, jnp.zeros_like(c)], axis=0).astype(BF16)
    cs = _dot(c16, tri_ref[...])[0:N_EXPERTS]
    rank_all = run_ref[:, 0:1] + cs - 1.0
    rank1 = jnp.sum(jnp.where(sel1, rank_all, 0.0), axis=0, keepdims=True)
    rank2 = jnp.sum(jnp.where(sel2, rank_all, 0.0), axis=0, keepdims=True)
    run_new = run_ref[...] + jnp.sum(c, axis=1, keepdims=True)
    run_ref[...] = run_new
    cnt_ref[...] = run_new
    route = jnp.zeros((N_EXPERTS, tm), F32)
    for k, row in enumerate((i1, i2, g1, g2, rank1, rank2)):
        route = jnp.where(eidx == float(k), row, route)
    route_ref[...] = route


def _mix_out(o_attn, u, h, lay, pool_w_bf, pool_scale, g_attn, g_pool, w_out_bf, ln_ffn,
             router_t=None):
    rows = h.shape[0]
    tm = TM_MIX
    moe = router_t is not None
    blk8 = tm // POOL_HALO
    n8 = rows // POOL_HALO
    row1 = lambda a: a.reshape(1, -1)
    in_specs = [
        pl.BlockSpec((tm, ATTN_WIDTH), lambda i: (i, 0)),
        pl.BlockSpec((tm, POOL_WIDTH), lambda i: (i, 0)),
        pl.BlockSpec((POOL_HALO, POOL_WIDTH), lambda i: (jnp.maximum(i * blk8 - 1, 0), 0)),
        pl.BlockSpec((POOL_HALO, POOL_WIDTH), lambda i: (jnp.minimum((i + 1) * blk8, n8 - 1), 0)),
        pl.BlockSpec((tm, 1), lambda i: (i, 0)),
        pl.BlockSpec((tm, 1), lambda i: (i, 0)),
        pl.BlockSpec((len(POOL_WINDOWS), POOL_GROUP, POOL_GROUP), lambda i: (0, 0, 0)),
        pl.BlockSpec((1, POOL_WIDTH), lambda i: (0, 0)),
        pl.BlockSpec((1, ATTN_WIDTH), lambda i: (0, 0)),
        pl.BlockSpec((1, POOL_WIDTH), lambda i: (0, 0)),
        pl.BlockSpec((D_MODEL, D_MODEL), lambda i: (0, 0), pipeline_mode=pl.Buffered(1)),
        pl.BlockSpec((tm, D_MODEL), lambda i: (i, 0)),
        pl.BlockSpec((1, D_MODEL), lambda i: (0, 0)),
    ]
    args = [o_attn, u, u, u, jnp.asarray(lay.pos), jnp.asarray(lay.rem), pool_w_bf,
            row1(pool_scale), row1(g_attn), row1(g_pool), w_out_bf, h, row1(ln_ffn)]
    out_specs = [pl.BlockSpec((tm, D_MODEL), lambda i: (i, 0)),
                 pl.BlockSpec((tm, D_MODEL), lambda i: (i, 0))]
    out_shape = [jax.ShapeDtypeStruct((rows, D_MODEL), F32),
                 jax.ShapeDtypeStruct((rows, D_MODEL), F32 if moe else BF16)]
    scratch = [pltpu.VMEM((tm + 2 * POOL_HALO, POOL_WIDTH), F32),
               pltpu.VMEM((tm + 2 * POOL_HALO, POOL_WIDTH), BF16),
               pltpu.VMEM((tm + 2 * POOL_HALO, POOL_WIDTH), BF16)]
    if moe:
        tri = jnp.asarray(np.triu(np.ones((tm, tm), np.float32)), BF16)
        in_specs += [pl.BlockSpec((32, D_MODEL), lambda i: (0, 0)),
                     pl.BlockSpec((tm, tm), lambda i: (0, 0))]
        args += [router_t, tri]
        out_specs += [pl.BlockSpec((N_EXPERTS, tm), lambda i: (0, i)),
                      pl.BlockSpec((N_EXPERTS, LANES), lambda i: (0, 0))]
        out_shape += [jax.ShapeDtypeStruct((N_EXPERTS, rows), F32),
                      jax.ShapeDtypeStruct((N_EXPERTS, LANES), F32)]
        scratch += [pltpu.VMEM((N_EXPERTS, LANES), F32)]
    return pl.pallas_call(
        functools.partial(_mix_out_kernel, moe=moe, n_valid=lay.n_valid),
        grid=(rows // tm,),
        in_specs=in_specs,
        out_specs=out_specs,
        out_shape=out_shape,
        scratch_shapes=scratch,
        compiler_params=pltpu.CompilerParams(
            dimension_semantics=("arbitrary",), vmem_limit_bytes=VMEM_LIMIT),
        name="mix_out_moe" if moe else "mix_out",
    )(*args)


def _swiglu_step(x, wg_ref, wu_ref, wd_ref):
    gate = _dot(x, wg_ref[...])
    up = _dot(x, wu_ref[...])
    mid = (gate / (1.0 + jnp.exp(-gate))) * up
    return _dot(mid.astype(BF16), wd_ref[...])


def _ffn_dense_kernel(x_ref, wg_ref, wu_ref, wd_ref, res_ref, o_ref):
    @pl.when(pl.program_id(1) == 0)
    def _():
        o_ref[...] = res_ref[...]

    o_ref[...] += _swiglu_step(x_ref[...], wg_ref, wu_ref, wd_ref)


def _ffn_dense(x, wg_bf, wu_bf, wd_bf, res):
    rows = x.shape[0]
    tm, tf = TM_FFN, TF_FFN
    return pl.pallas_call(
        _ffn_dense_kernel,
        grid=(rows // tm, D_FF // tf),
        in_specs=[
            pl.BlockSpec((tm, D_MODEL), lambda i, j: (i, 0)),
            pl.BlockSpec((D_MODEL, tf), lambda i, j: (0, j)),
            pl.BlockSpec((D_MODEL, tf), lambda i, j: (0, j)),
            pl.BlockSpec((tf, D_MODEL), lambda i, j: (j, 0)),
            pl.BlockSpec((tm, D_MODEL), lambda i, j: (i, 0)),
        ],
        out_specs=pl.BlockSpec((tm, D_MODEL), lambda i, j: (i, 0)),
        out_shape=jax.ShapeDtypeStruct((rows, D_MODEL), F32),
        compiler_params=pltpu.CompilerParams(
            dimension_semantics=("arbitrary", "arbitrary"), vmem_limit_bytes=VMEM_LIMIT),
        name="ffn_dense",
    )(x, wg_bf, wu_bf, wd_bf, res)


def _ffn_moe_kernel(te_ref, nu_ref, src_ref, hn_ref, wg_ref, wu_ref, wd_ref, o_ref,
                    xbuf_ref, xb_ref, sem):
    del te_ref
    tm = o_ref.shape[0]
    n_tiles = pl.num_programs(0)
    nf = pl.num_programs(1)
    share = xbuf_ref.shape[1] // D_FF_STEPS
    i = pl.program_id(0)
    j = pl.program_id(1)
    n_used = nu_ref[0]
    used = i < n_used
    slot = lax.rem(i, 2)

    def row_copy(tile, row, buf):
        return pltpu.make_async_copy(hn_ref.at[pl.ds(src_ref[tile * tm + row], 1)],
                                     xbuf_ref.at[buf, pl.ds(row, 1)], sem.at[buf])

    def tile_rows(buf):
        return pltpu.make_async_copy(hn_ref.at[pl.ds(0, xbuf_ref.shape[1])], xbuf_ref.at[buf],
                                     sem.at[buf])

    @pl.when(j == 0)
    def _():
        @pl.when(i == 0)
        def _():
            def body(r, c):
                row_copy(0, r, 0).start()
                return c

            lax.fori_loop(0, xbuf_ref.shape[1], body, 0, unroll=8)

        @pl.when(i <= n_used)
        def _():
            tile_rows(slot).wait()

        @pl.when(used)
        def _():
            xb_ref[...] = xbuf_ref[slot, 0:tm, :].astype(BF16)

        o_ref[...] = jnp.zeros_like(o_ref)

    @pl.when(used)
    def _():
        for k in range(share):
            row_copy(i + 1, j * share + k, 1 - slot).start()
        o_ref[...] += _swiglu_step(xb_ref[...], wg_ref, wu_ref, wd_ref)

    @pl.when(jnp.logical_and(used, jnp.logical_and(i == n_tiles - 1, j == nf - 1)))
    def _():
        tile_rows(1 - slot).wait()


def _ffn_moe(hn, src, wg_bf, wu_bf, wd_bf, tile_expert, n_used, lay):
    tm, tf = TM_FFN, TF_FFN
    nf = D_FF // tf

    def jj(i, j, nu):
        return jnp.where(i < nu[0], j, nf - 1)

    grid_spec = pltpu.PrefetchScalarGridSpec(
        num_scalar_prefetch=3,
        grid=(lay.slots // tm, nf),
        in_specs=[
            pl.BlockSpec(memory_space=pl.ANY),
            pl.BlockSpec((None, D_MODEL, tf), lambda i, j, te, nu, sr: (te[i], 0, jj(i, j, nu))),
            pl.BlockSpec((None, D_MODEL, tf), lambda i, j, te, nu, sr: (te[i], 0, jj(i, j, nu))),
            pl.BlockSpec((None, tf, D_MODEL), lambda i, j, te, nu, sr: (te[i], jj(i, j, nu), 0)),
        ],
        out_specs=pl.BlockSpec((tm, D_MODEL), lambda i, j, te, nu, sr: (i, 0)),
        scratch_shapes=[
            pltpu.VMEM((2, GATHER_ROWS, D_MODEL), F32),
            pltpu.VMEM((tm, D_MODEL), BF16),
            pltpu.SemaphoreType.DMA((2,)),
        ],
    )
    return pl.pallas_call(
        _ffn_moe_kernel,
        grid_spec=grid_spec,
        out_shape=jax.ShapeDtypeStruct((lay.slots, D_MODEL), F32),
        compiler_params=pltpu.CompilerParams(
            dimension_semantics=("arbitrary", "arbitrary"), vmem_limit_bytes=VMEM_LIMIT),
        name="ffn_moe",
    )(tile_expert, n_used, src, hn, wg_bf, wu_bf, wd_bf)


def _slot_src_kernel(pos_ref, src_ref, *, td, slots):
    i = pl.program_id(0)

    @pl.when(i == 0)
    def _():
        def init(s, c):
            src_ref[s] = 0
            return c

        lax.fori_loop(0, slots, init, 0, unroll=8)

    base = i * td

    def body(r, c):
        src_ref[pos_ref[0, r]] = base + r
        src_ref[pos_ref[1, r]] = base + r
        return c

    lax.fori_loop(0, td, body, 0, unroll=8)


def _slot_sources(pos3, lay):
    td = lay.td
    return pl.pallas_call(
        functools.partial(_slot_src_kernel, td=td, slots=lay.src_len),
        grid=(lay.n_valid // td,),
        in_specs=[pl.BlockSpec((None, 2, td), lambda i: (i, 0, 0), memory_space=pltpu.SMEM)],
        out_specs=pl.BlockSpec(memory_space=pltpu.SMEM),
        out_shape=jax.ShapeDtypeStruct((lay.src_len,), jnp.int32),
        compiler_params=pltpu.CompilerParams(dimension_semantics=("arbitrary",)),
        name="slot_sources",
    )(pos3)


def _combine_kernel(pos_ref, nxt_ref, gate_ref, h_ref, gf_ref, y_ref, o_ref, buf_ref, sem, *, tc):
    t = pl.program_id(0)
    slot = lax.rem(t, 2)

    def start_rows(idx_ref, buf):
        def issue(r, c):
            for k in range(2):
                pltpu.make_async_copy(y_ref.at[pl.ds(idx_ref[k, r], 1)],
                                      buf_ref.at[buf, k, pl.ds(r, 1)], sem.at[buf]).start()
            return c

        lax.fori_loop(0, tc, issue, 0, unroll=8)

    @pl.when(t == 0)
    def _():
        start_rows(pos_ref, 0)

    @pl.when(t + 1 < pl.num_programs(0))
    def _():
        start_rows(nxt_ref, 1 - slot)

    for k in range(2):
        pltpu.make_async_copy(y_ref.at[pl.ds(0, tc)], buf_ref.at[slot, k], sem.at[slot]).wait()
    gate = gate_ref[...]
    y = gate[:, 0:1] * buf_ref[slot, 0] + gate[:, 1:2] * buf_ref[slot, 1]
    o_ref[...] = _rms(h_ref[...] + y, gf_ref[...]).astype(o_ref.dtype)


def _combine(h1, y_sorted, pos, gates, g_final, row0, batch, tokens):
    tc = TM_OUT
    per = tokens // tc
    el = pl.Element
    seq = N_META + tokens

    def grid_rows(a):
        a = a[:, row0:row0 + batch * seq].reshape(a.shape[0], batch, seq)[:, :, N_META:]
        return a.reshape(a.shape[0], batch * tokens)

    pos_g = grid_rows(pos).reshape(2, batch * per, tc).transpose(1, 0, 2)
    gates_g = grid_rows(gates).T

    def rmap(t):
        return (pl.multiple_of(row0 + N_META + (t // per) * seq + (t % per) * tc, ROW_MULT), 0)

    return pl.pallas_call(
        functools.partial(_combine_kernel, tc=tc),
        grid=(batch * per,),
        in_specs=[
            pl.BlockSpec((None, 2, tc), lambda t: (t, 0, 0), memory_space=pltpu.SMEM),
            pl.BlockSpec((None, 2, tc), lambda t: (jnp.minimum(t + 1, batch * per - 1), 0, 0),
                         memory_space=pltpu.SMEM),
            pl.BlockSpec((tc, 2), lambda t: (t, 0)),
            pl.BlockSpec((el(tc), el(D_MODEL)), rmap),
            pl.BlockSpec((1, D_MODEL), lambda t: (0, 0)),
            pl.BlockSpec(memory_space=pl.ANY),
        ],
        out_specs=pl.BlockSpec((None, tc, D_MODEL), lambda t: (t // per, t % per, 0)),
        out_shape=jax.ShapeDtypeStruct((batch, tokens, D_MODEL), F32),
        scratch_shapes=[pltpu.VMEM((2, 2, tc, D_MODEL), F32), pltpu.SemaphoreType.DMA((2,))],
        compiler_params=pltpu.CompilerParams(
            dimension_semantics=("arbitrary",), vmem_limit_bytes=VMEM_LIMIT),
        name="combine",
    )(pos_g, pos_g, gates_g, h1, g_final.reshape(1, D_MODEL), y_sorted)


def _routing_tables(route, cnt, lay):
    tmx = TM_FFN
    counts = cnt[:, 0].astype(jnp.int32)
    padded = (counts + tmx - 1) // tmx * tmx
    ends = jnp.cumsum(padded)
    offs = ends - padded
    e_idx = route[0:2].astype(jnp.int32)
    rank = route[4:6].astype(jnp.int32)
    onehot = e_idx[:, :, None] == jnp.arange(N_EXPERTS, dtype=jnp.int32)
    pos = jnp.sum(jnp.where(onehot, offs, 0), axis=-1) + rank
    pos = jnp.where(jnp.arange(pos.shape[1])[None, :] < lay.n_valid, pos, 0)
    n_tiles = lay.slots // tmx
    tile_start = jnp.arange(n_tiles, dtype=jnp.int32) * tmx
    tile_expert = jnp.minimum(jnp.sum(tile_start[:, None] >= ends[None, :], axis=1),
                              N_EXPERTS - 1).astype(jnp.int32)
    n_used = (ends[-1] // tmx).reshape(1).astype(jnp.int32)
    return pos, route[2:4], tile_expert, n_used


def _forward(xs, meta_tokens, ln_mix, w_in, rpb, meta_bias, pool_w, pool_scale, g_attn_out,
             g_pool_out, w_out, ln_ffn, ffn_w_gate, ffn_w_up, ffn_w_down, router, moe_w_gate,
             moe_w_up, moe_w_down, g_final):
    groups = [(x.shape[0], x.shape[1]) for x in xs]
    lay = _Layout(groups)
    depth = ln_mix.shape[0]
    assert depth == 2

    parts = []
    for x in xs:
        for b in range(x.shape[0]):
            parts += [meta_tokens.astype(F32), x[b]]
    parts.append(jnp.zeros((lay.rows - lay.n_valid, D_MODEL), F32))
    h = jnp.concatenate(parts, axis=0)

    outs = None
    for layer in range(depth):
        qkv, u = _in_proj(h, ln_mix[layer], w_in[layer].astype(BF16))
        bias8, mbias = _attn_bias_tables(rpb[layer], meta_bias[layer])
        o_attn = jnp.zeros((lay.rows, ATTN_WIDTH), BF16)
        o_attn = _attention_grid(qkv, o_attn, bias8, mbias, lay)
        o_attn = _attention_meta(qkv, o_attn, meta_bias[layer].astype(F32), lay)
        mix_args = (o_attn, u, h, lay, pool_w[layer].astype(BF16), pool_scale[layer],
                    g_attn_out[layer], g_pool_out[layer], w_out[layer].astype(BF16), ln_ffn[layer])
        i = layer // 2
        if layer % 2 == 0:
            h1, hn = _mix_out(*mix_args)
            h = _ffn_dense(hn, ffn_w_gate[i].astype(BF16), ffn_w_up[i].astype(BF16),
                           ffn_w_down[i].astype(BF16), h1)
        else:
            rt = router[i].T
            rt_hi = rt.astype(BF16)
            rt_lo = (rt - rt_hi.astype(F32)).astype(BF16)
            zpad = jnp.zeros_like(rt_hi)
            router_t = jnp.concatenate([rt_hi, zpad, rt_lo, zpad], axis=0)
            h1, hn, route, cnt = _mix_out(*mix_args, router_t=router_t)
            pos, gates, tile_expert, n_used = _routing_tables(route, cnt, lay)
            td = lay.td
            pos3 = pos[:, :lay.n_valid].reshape(2, lay.n_valid // td, td).transpose(1, 0, 2)
            src = _slot_sources(pos3, lay)
            y_sorted = _ffn_moe(hn, src, moe_w_gate[i].astype(BF16), moe_w_up[i].astype(BF16),
                                moe_w_down[i].astype(BF16), tile_expert, n_used, lay)
            outs = []
            seq = 0
            for b, t in groups:
                outs.append(_combine(h1, y_sorted, pos, gates, g_final, lay.seq_start[seq], b, t))
                seq += b
    return tuple(outs)


def kernel(x_prompt, x_sample, meta_tokens, ln_mix, w_in, rpb, meta_bias, pool_w, pool_scale,
           g_attn_out, g_pool_out, w_out, ln_ffn, ffn_w_gate, ffn_w_up, ffn_w_down, router,
           moe_w_gate, moe_w_up, moe_w_down, g_final):
    return _forward((x_prompt, x_sample), meta_tokens, ln_mix, w_in, rpb, meta_bias, pool_w,
                    pool_scale, g_attn_out, g_pool_out, w_out, ln_ffn, ffn_w_gate, ffn_w_up,
                    ffn_w_down, router, moe_w_gate, moe_w_up, moe_w_down, g_final)
```

```python
import functools

import numpy as np
import jax
import jax.numpy as jnp
from jax import lax
from jax.experimental import pallas as pl
from jax.experimental.pallas import tpu as pltpu

F32 = jnp.float32
BF16 = jnp.bfloat16

D_MODEL = 2048
GRID_W = 64
N_META = 16
ATTN_HEADS = 16
HEAD_DIM = 64
ATTN_WIDTH = ATTN_HEADS * HEAD_DIM
POOL_WIDTH = D_MODEL - ATTN_WIDTH
POOL_WINDOWS = (2, 4, 8, 16)
POOL_GROUP = POOL_WIDTH // len(POOL_WINDOWS)
POOL_HALO = max(POOL_WINDOWS) // 2
NA_ROWS = 8
NA_COLS = 16
D_FF = 5632
N_EXPERTS = 8
EPS = 1e-6
NEG = -1e30

LANES = 128
HEAD_PAIR = LANES // HEAD_DIM
VMEM_LIMIT = 56 * 1024 * 1024

TM_IN = 1024
TN_IN = 1024
TM_MIX = 512
MIX_SUB = 512
POOL_SUB = 128
TM_FFN = 512
TF_FFN = 512
D_FF_STEPS = D_FF // TF_FFN
GATHER_SHARE = (-(-TM_FFN // D_FF_STEPS) + 7) // 8 * 8
GATHER_ROWS = GATHER_SHARE * D_FF_STEPS
TM_OUT = 256
ATTN_CHUNK = 32
ATTN_BLOCK = 32
ROW_ALIGN = 1024
ROW_MULT = 16


def _round_up(x, m):
    return (x + m - 1) // m * m


def _rms(x, g):
    return x * lax.rsqrt(jnp.mean(x * x, axis=-1, keepdims=True) + EPS) * g


def _dot(a, b):
    return jnp.dot(a, b, preferred_element_type=F32)


def _dot_nt(a, b):
    return lax.dot_general(a, b, (((1,), (1,)), ((), ())), preferred_element_type=F32)


class _Layout:
    def __init__(self, groups):
        self.groups = groups
        self.seq_start, self.seq_tokens = [], []
        row = 0
        for b, t in groups:
            assert t % (GRID_W * NA_ROWS) == 0
            for _ in range(b):
                self.seq_start.append(row)
                self.seq_tokens.append(t)
                row += N_META + t
        self.n_valid = row
        self.rows = _round_up(row, ROW_ALIGN)
        pos = np.zeros((self.rows, 1), np.int32)
        rem = np.ones((self.rows, 1), np.int32)
        for s, t in zip(self.seq_start, self.seq_tokens):
            n = N_META + t
            pos[s:s + n, 0] = np.arange(n)
            rem[s:s + n, 0] = n - np.arange(n)
        self.pos, self.rem = pos, rem
        items = []
        self.chunk = ATTN_CHUNK
        self.kwin = self.chunk + NA_ROWS
        for s, t in zip(self.seq_start, self.seq_tokens):
            g = t // GRID_W
            assert g % self.chunk == 0
            for c in range(g // self.chunk):
                r_base = c * self.chunk
                kv_row0 = int(np.clip(r_base - NA_ROWS // 2, 0, max(g - self.kwin, 0)))
                items.append((s + N_META + r_base * GRID_W, s + N_META + kv_row0 * GRID_W,
                              s, r_base, g, kv_row0))
        self.items = np.asarray(items, np.int32)
        assert int(self.items[:, 1].max()) + self.kwin * GRID_W <= self.rows
        self.td = max(d for d in range(8, 513, 8) if self.n_valid % d == 0)
        self.slots = _round_up(2 * self.n_valid + N_EXPERTS * (TM_FFN - 1), TM_FFN)
        self.src_len = self.slots + GATHER_ROWS


def _in_proj_kernel(x_ref, g_ref, w_ref, qkv_ref, u_ref, xn_ref):
    j = pl.program_id(1)

    @pl.when(j == 0)
    def _():
        xn_ref[...] = _rms(x_ref[...], g_ref[...]).astype(BF16)

    z = _dot(xn_ref[...], w_ref[...])

    @pl.when(j == 0)
    def _():
        qkv_ref[...] = (z * (HEAD_DIM ** -0.5)).astype(BF16)

    @pl.when(jnp.logical_and(j > 0, j < 3))
    def _():
        qkv_ref[...] = z.astype(BF16)

    @pl.when(j == 3)
    def _():
        u_ref[...] = z


def _in_proj(h, g, w_bf):
    rows = h.shape[0]
    assert TN_IN == ATTN_WIDTH == POOL_WIDTH
    return pl.pallas_call(
        _in_proj_kernel,
        grid=(rows // TM_IN, 4),
        in_specs=[
            pl.BlockSpec((TM_IN, D_MODEL), lambda i, j: (i, 0)),
            pl.BlockSpec((1, D_MODEL), lambda i, j: (0, 0)),
            pl.BlockSpec((D_MODEL, TN_IN), lambda i, j: (0, j)),
        ],
        out_specs=[
            pl.BlockSpec((TM_IN, TN_IN), lambda i, j: (i, jnp.minimum(j, 2))),
            pl.BlockSpec((TM_IN, TN_IN), lambda i, j: (i, 0)),
        ],
        out_shape=[
            jax.ShapeDtypeStruct((rows, 3 * ATTN_WIDTH), BF16),
            jax.ShapeDtypeStruct((rows, POOL_WIDTH), F32),
        ],
        scratch_shapes=[pltpu.VMEM((TM_IN, D_MODEL), BF16)],
        compiler_params=pltpu.CompilerParams(
            dimension_semantics=("arbitrary", "arbitrary"), vmem_limit_bytes=VMEM_LIMIT),
        name="in_proj",
    )(h, g.reshape(1, D_MODEL), w_bf)


def _attn_kernel(qoff_ref, kvoff_ref, moff_ref, rbase_ref, gsz_ref, kvrow_ref,
                 q_ref, k_ref, v_ref, km_ref, vm_ref, bias_ref, mb_ref, o_ref,
                 ka_ref, kb_ref, kme_ref, vme_ref, s_ref, p_ref, l_ref, m_ref, *, chunk):
    it = pl.program_id(1)
    r_base = rbase_ref[it]
    g_rows = gsz_ref[it]
    kv_row0 = kvrow_ref[it]
    is_a = lax.broadcasted_iota(jnp.int32, (1, LANES), 1) < HEAD_DIM

    k = k_ref[...]
    kz = jnp.zeros_like(k)
    ka_ref[...] = jnp.where(is_a, k, kz)
    kb_ref[...] = jnp.where(is_a, kz, k)
    km = km_ref[...]
    kmz = jnp.zeros_like(km)
    kme_ref[...] = jnp.zeros_like(kme_ref)
    kme_ref[0:N_META, :] = jnp.where(is_a, km, kmz)
    kme_ref[N_META:2 * N_META, :] = jnp.where(is_a, kmz, km)
    vme_ref[...] = jnp.zeros_like(vme_ref)
    vme_ref[0:N_META, :] = vm_ref[...]
    vme_ref[N_META:2 * N_META, :] = vm_ref[...]

    nk = NA_ROWS * GRID_W
    krefs = (ka_ref, kb_ref)

    def row_window(r):
        rg = r_base + r
        rs = jnp.clip(rg - NA_ROWS // 2, 0, g_rows - NA_ROWS)
        var = rs - rg + (NA_ROWS - 1)
        kl = pl.multiple_of((rs - kv_row0) * GRID_W, GRID_W)
        q0 = pl.multiple_of(r * GRID_W, GRID_W)
        return var, kl, q0

    def block(bi, carry):
        for rr in range(ATTN_BLOCK):
            var, kl, q0 = row_window(bi * ATTN_BLOCK + rr)
            rows = slice(rr * GRID_W, (rr + 1) * GRID_W)
            q = q_ref[pl.ds(q0, GRID_W), :]
            s_meta = _dot_nt(q, kme_ref[...])
            for hd in range(HEAD_PAIR):
                sc = _dot_nt(q, krefs[hd][pl.ds(kl, nk), :]) + bias_ref[hd, var]
                sm = s_meta + mb_ref[hd]
                s_ref[hd, rows, 0:nk] = sc
                s_ref[hd, rows, nk:nk + LANES] = sm
                for c in range(nk // LANES):
                    sm = jnp.maximum(sm, sc[:, c * LANES:(c + 1) * LANES])
                m_ref[hd, rows, :] = sm
        for hd in range(HEAD_PAIR):
            for rr in range(ATTN_BLOCK):
                rows = slice(rr * GRID_W, (rr + 1) * GRID_W)
                m = jnp.max(m_ref[hd, rows, :], axis=1, keepdims=True)
                acc = None
                for c in range(nk // LANES + 1):
                    cols = slice(c * LANES, (c + 1) * LANES)
                    p = jnp.exp(s_ref[hd, rows, cols] - m)
                    acc = p if acc is None else acc + p
                    p_ref[hd, rows, cols] = p.astype(BF16)
                inv = 1.0 / jnp.sum(acc, axis=1, keepdims=True)
                l_ref[hd, rows, :] = jnp.broadcast_to(inv, (GRID_W, LANES))
        for rr in range(ATTN_BLOCK):
            var, kl, q0 = row_window(bi * ATTN_BLOCK + rr)
            rows = slice(rr * GRID_W, (rr + 1) * GRID_W)
            vv = v_ref[pl.ds(kl, nk), :]
            outs = []
            for hd in range(HEAD_PAIR):
                o = (_dot(p_ref[hd, rows, 0:nk], vv)
                     + _dot(p_ref[hd, rows, nk:nk + LANES], vme_ref[...]))
                outs.append(o * l_ref[hd, rows, :])
            o_ref[pl.ds(q0, GRID_W), :] = jnp.where(is_a, outs[0], outs[1]).astype(o_ref.dtype)
        return carry

    lax.fori_loop(0, chunk // ATTN_BLOCK, block, 0)


def _attention_grid(qkv, o_init, bias8, mbias, lay):
    rows = qkv.shape[0]
    qrows = lay.chunk * GRID_W
    krows = lay.kwin * GRID_W
    n_items = lay.items.shape[0]
    n_pairs = ATTN_HEADS // HEAD_PAIR
    tabs = [jnp.asarray(lay.items[:, c]) for c in range(6)]
    el = pl.Element

    def qmap(hp, it, qo, ko, mo, rb, gs, kr):
        return (pl.multiple_of(qo[it], ROW_MULT), pl.multiple_of(hp * LANES, LANES))

    def kmap(col0):
        def f(hp, it, qo, ko, mo, rb, gs, kr):
            return (pl.multiple_of(ko[it], ROW_MULT), pl.multiple_of(col0 + hp * LANES, LANES))
        return f

    def mmap(col0):
        def f(hp, it, qo, ko, mo, rb, gs, kr):
            return (pl.multiple_of(mo[it], ROW_MULT), pl.multiple_of(col0 + hp * LANES, LANES))
        return f

    grid_spec = pltpu.PrefetchScalarGridSpec(
        num_scalar_prefetch=6,
        grid=(n_pairs, n_items),
        in_specs=[
            pl.BlockSpec((el(qrows), el(LANES)), qmap),
            pl.BlockSpec((el(krows), el(LANES)), kmap(ATTN_WIDTH)),
            pl.BlockSpec((el(krows), el(LANES)), kmap(2 * ATTN_WIDTH)),
            pl.BlockSpec((el(N_META), el(LANES)), mmap(ATTN_WIDTH)),
            pl.BlockSpec((el(N_META), el(LANES)), mmap(2 * ATTN_WIDTH)),
            pl.BlockSpec((HEAD_PAIR, NA_ROWS, GRID_W, NA_ROWS * GRID_W),
                         lambda hp, it, *_: (hp, 0, 0, 0)),
            pl.BlockSpec((HEAD_PAIR, 1, LANES), lambda hp, it, *_: (hp, 0, 0)),
            pl.BlockSpec(memory_space=pl.ANY),
        ],
        out_specs=pl.BlockSpec((el(qrows), el(LANES)), qmap),
        scratch_shapes=[
            pltpu.VMEM((krows, LANES), BF16),
            pltpu.VMEM((krows, LANES), BF16),
            pltpu.VMEM((LANES, LANES), BF16),
            pltpu.VMEM((LANES, LANES), BF16),
            pltpu.VMEM((HEAD_PAIR, ATTN_BLOCK * GRID_W, NA_ROWS * GRID_W + LANES), F32),
            pltpu.VMEM((HEAD_PAIR, ATTN_BLOCK * GRID_W, NA_ROWS * GRID_W + LANES), BF16),
            pltpu.VMEM((HEAD_PAIR, ATTN_BLOCK * GRID_W, LANES), F32),
            pltpu.VMEM((HEAD_PAIR, ATTN_BLOCK * GRID_W, LANES), F32),
        ],
    )

    def kern(qo, ko, mo, rb, gs, kr, q, k, v, km, vm, b, mb, o_in, o, *scr):
        del o_in
        _attn_kernel(qo, ko, mo, rb, gs, kr, q, k, v, km, vm, b, mb, o, *scr, chunk=lay.chunk)

    return pl.pallas_call(
        kern,
        grid_spec=grid_spec,
        out_shape=jax.ShapeDtypeStruct((rows, ATTN_WIDTH), BF16),
        input_output_aliases={13: 0},
        compiler_params=pltpu.CompilerParams(
            dimension_semantics=("arbitrary", "arbitrary"), vmem_limit_bytes=VMEM_LIMIT),
        name="attn_grid",
    )(*tabs, qkv, qkv, qkv, qkv, qkv, bias8, mbias, o_init)


def _attn_meta_kernel(moff_ref, x_ref, mb_ref, o_in_ref, o_ref):
    del moff_ref, o_in_ref
    x = x_ref[...].astype(F32)
    outs = []
    for h in range(ATTN_HEADS):
        q = x[:, h * HEAD_DIM:(h + 1) * HEAD_DIM].astype(BF16)
        k = x[:, ATTN_WIDTH + h * HEAD_DIM:ATTN_WIDTH + (h + 1) * HEAD_DIM].astype(BF16)
        v = x[:, 2 * ATTN_WIDTH + h * HEAD_DIM:2 * ATTN_WIDTH + (h + 1) * HEAD_DIM].astype(BF16)
        s = _dot_nt(q, k) + mb_ref[h:h + 1, :]
        m = jnp.max(s, axis=1, keepdims=True)
        p = jnp.exp(s - m)
        l = jnp.sum(p, axis=1, keepdims=True)
        outs.append(_dot(p.astype(BF16), v) / l)
    o_ref[...] = jnp.concatenate(outs, axis=1).astype(o_ref.dtype)


def _attention_meta(qkv, o_init, meta_bias, lay):
    rows = qkv.shape[0]
    n_seq = len(lay.seq_start)
    moff = jnp.asarray(np.asarray(lay.seq_start, np.int32))
    el = pl.Element
    grid_spec = pltpu.PrefetchScalarGridSpec(
        num_scalar_prefetch=1,
        grid=(n_seq,),
        in_specs=[
            pl.BlockSpec((el(N_META), el(3 * ATTN_WIDTH)),
                         lambda b, mo: (pl.multiple_of(mo[b], ROW_MULT), 0)),
            pl.BlockSpec((ATTN_HEADS, N_META), lambda b, mo: (0, 0)),
            pl.BlockSpec(memory_space=pl.ANY),
        ],
        out_specs=pl.BlockSpec((el(N_META), el(ATTN_WIDTH)),
                               lambda b, mo: (pl.multiple_of(mo[b], ROW_MULT), 0)),
    )
    return pl.pallas_call(
        _attn_meta_kernel,
        grid_spec=grid_spec,
        out_shape=jax.ShapeDtypeStruct((rows, ATTN_WIDTH), BF16),
        input_output_aliases={3: 0},
        compiler_params=pltpu.CompilerParams(dimension_semantics=("arbitrary",)),
        name="attn_meta",
    )(moff, qkv, meta_bias, o_init)


def _attn_bias_tables(rpb, meta_bias):
    cq = np.arange(GRID_W)
    cs = np.clip(cq - NA_COLS // 2, 0, GRID_W - NA_COLS)
    ck = np.arange(GRID_W)
    valid = (ck[None, :] >= cs[:, None]) & (ck[None, :] < cs[:, None] + NA_COLS)
    dc = ck[None, :] - cq[:, None] + NA_COLS - 1
    onehot = (np.arange(2 * NA_COLS - 1)[:, None, None] == dc[None]) & valid[None]
    t = jnp.einsum('hrd,dqk->hqrk', rpb.astype(F32), jnp.asarray(onehot, F32),
                   precision=lax.Precision.HIGHEST)
    t = t + jnp.asarray(np.where(valid, 0.0, NEG), F32)[None, :, None, :]
    bias8 = jnp.stack([t[:, :, v:v + NA_ROWS, :].reshape(ATTN_HEADS, GRID_W, NA_ROWS * GRID_W)
                       for v in range(NA_ROWS)], axis=1)
    lane = np.arange(LANES)
    head = np.arange(ATTN_HEADS)
    slot = (head % HEAD_PAIR) * N_META
    src = np.clip(lane[None, :] - slot[:, None], 0, N_META - 1)
    ok = (lane[None, :] >= slot[:, None]) & (lane[None, :] < slot[:, None] + N_META)
    mb = jnp.where(jnp.asarray(ok), jnp.take_along_axis(meta_bias.astype(F32), jnp.asarray(src), 1), NEG)
    return bias8, mb.reshape(ATTN_HEADS, 1, LANES)


def _mix_out_kernel(*refs, moe, n_valid):
    if moe:
        (oa_ref, u_ref, up_ref, un_ref, pos_ref, rem_ref, pw_ref, ps_ref, ga_ref, gp_ref, wo_ref,
         h_ref, lf_ref, rt_ref, tri_ref, h1_ref, hn_ref, route_ref, cnt_ref, ext_ref, hi_ref, lo_ref,
         run_ref) = refs
    else:
        (oa_ref, u_ref, up_ref, un_ref, pos_ref, rem_ref, pw_ref, ps_ref, ga_ref, gp_ref, wo_ref,
         h_ref, lf_ref, h1_ref, hn_ref, ext_ref, hi_ref, lo_ref) = refs
    tm = u_ref.shape[0]
    hal = POOL_HALO
    ext_ref[0:hal, :] = up_ref[...]
    ext_ref[hal:hal + tm, :] = u_ref[...]
    ext_ref[hal + tm:hal + tm + hal, :] = un_ref[...]
    ext = ext_ref[...]
    ext_hi = ext.astype(BF16)
    hi_ref[...] = ext_hi
    lo_ref[...] = (ext - ext_hi.astype(F32)).astype(BF16)
    band_rows = POOL_SUB + 2 * hal
    dist = (lax.broadcasted_iota(jnp.int32, (POOL_SUB, band_rows), 1)
            - lax.broadcasted_iota(jnp.int32, (POOL_SUB, band_rows), 0) - hal)

    for r0 in range(0, tm, MIX_SUB):
        rows = slice(r0, r0 + MIX_SUB)
        mixed = []
        for g, w in enumerate(POOL_WINDOWS):
            half = w // 2
            cols = slice(g * POOL_GROUP, (g + 1) * POOL_GROUP)
            slabs = []
            for p0 in range(r0, r0 + MIX_SUB, POOL_SUB):
                back = jnp.minimum(pos_ref[p0:p0 + POOL_SUB, :], half)
                fwd = jnp.minimum(rem_ref[p0:p0 + POOL_SUB, :], half)
                band = jnp.where(jnp.logical_and(dist >= -back, dist < fwd), 1.0, 0.0).astype(BF16)
                win = (_dot(band, hi_ref[p0:p0 + band_rows, cols])
                       + _dot(band, lo_ref[p0:p0 + band_rows, cols]))
                mean = win / (back + fwd).astype(F32)
                slabs.append(mean - ext_ref[hal + p0:hal + p0 + POOL_SUB, cols])
            pooled = jnp.concatenate(slabs, axis=0).astype(BF16)
            mixed.append(_dot(pooled, pw_ref[g]))
        o_pool = jnp.concatenate(mixed, axis=1) * ps_ref[...]
        n_pool = _rms(o_pool, gp_ref[...]).astype(BF16)
        n_attn = _rms(oa_ref[rows, :].astype(F32), ga_ref[...]).astype(BF16)
        mix = _dot(n_attn, wo_ref[0:ATTN_WIDTH, :]) + _dot(n_pool, wo_ref[ATTN_WIDTH:D_MODEL, :])
        h1 = h_ref[rows, :] + mix
        h1_ref[rows, :] = h1
        hn = _rms(h1, lf_ref[...])
        hn_ref[rows, :] = hn.astype(hn_ref.dtype)
    if not moe:
        return

    hn = hn_ref[...]
    i = pl.program_id(0)

    @pl.when(i == 0)
    def _():
        run_ref[...] = jnp.zeros_like(run_ref)

    hi = hn.astype(BF16)
    lo = (hn - hi.astype(F32)).astype(BF16)
    l1 = _dot_nt(rt_ref[...], hi)
    l2 = _dot_nt(rt_ref[0:16, :], lo)
    lg = l1[0:N_EXPERTS] + l1[16:16 + N_EXPERTS] + l2[0:N_EXPERTS]
    eidx = lax.broadcasted_iota(jnp.int32, (N_EXPERTS, tm), 0).astype(F32)
    none = float(N_EXPERTS)
    m1 = jnp.max(lg, axis=0, keepdims=True)
    i1 = jnp.min(jnp.where(lg == m1, eidx, none), axis=0, keepdims=True)
    sel1 = eidx == i1
    lg2 = jnp.where(sel1, -jnp.inf, lg)
    m2 = jnp.max(lg2, axis=0, keepdims=True)
    i2 = jnp.min(jnp.where(lg2 == m2, eidx, none), axis=0, keepdims=True)
    sel2 = eidx == i2
    t = jnp.exp(m2 - m1)
    g1 = 1.0 / (1.0 + t)
    g2 = t / (1.0 + t)
    rowid = i * tm + lax.broadcasted_iota(jnp.int32, (1, tm), 1)
    valid = rowid < n_valid
    c = jnp.where(jnp.logical_and(jnp.logical_or(sel1, sel2), valid), 1.0, 0.0)
    c16 = jnp.concatenate([c, jnp.zeros_like(c)], axis=0).astype(BF16)
    cs = _dot(c16, tri_ref[...])[0:N_EXPERTS]
    rank_all = run_ref[:, 0:1] + cs - 1.0
    rank1 = jnp.sum(jnp.where(sel1, rank_all, 0.0), axis=0, keepdims=True)
    rank2 = jnp.sum(jnp.where(sel2, rank_all, 0.0), axis=0, keepdims=True)
    run_new = run_ref[...] + jnp.sum(c, axis=1, keepdims=True)
    run_ref[...] = run_new
    cnt_ref[...] = run_new
    route = jnp.zeros((N_EXPERTS, tm), F32)
    for k, row in enumerate((i1, i2, g1, g2, rank1, rank2)):
        route = jnp.where(eidx == float(k), row, route)
    route_ref[...] = route


def _mix_out(o_attn, u, h, lay, pool_w_bf, pool_scale, g_attn, g_pool, w_out_bf, ln_ffn,
             router_t=None):
    rows = h.shape[0]
    tm = TM_MIX
    moe = router_t is not None
    blk8 = tm // POOL_HALO
    n8 = rows // POOL_HALO
    row1 = lambda a: a.reshape(1, -1)
    in_specs = [
        pl.BlockSpec((tm, ATTN_WIDTH), lambda i: (i, 0)),
        pl.BlockSpec((tm, POOL_WIDTH), lambda i: (i, 0)),
        pl.BlockSpec((POOL_HALO, POOL_WIDTH), lambda i: (jnp.maximum(i * blk8 - 1, 0), 0)),
        pl.BlockSpec((POOL_HALO, POOL_WIDTH), lambda i: (jnp.minimum((i + 1) * blk8, n8 - 1), 0)),
        pl.BlockSpec((tm, 1), lambda i: (i, 0)),
        pl.BlockSpec((tm, 1), lambda i: (i, 0)),
        pl.BlockSpec((len(POOL_WINDOWS), POOL_GROUP, POOL_GROUP), lambda i: (0, 0, 0)),
        pl.BlockSpec((1, POOL_WIDTH), lambda i: (0, 0)),
        pl.BlockSpec((1, ATTN_WIDTH), lambda i: (0, 0)),
        pl.BlockSpec((1, POOL_WIDTH), lambda i: (0, 0)),
        pl.BlockSpec((D_MODEL, D_MODEL), lambda i: (0, 0), pipeline_mode=pl.Buffered(1)),
        pl.BlockSpec((tm, D_MODEL), lambda i: (i, 0)),
        pl.BlockSpec((1, D_MODEL), lambda i: (0, 0)),
    ]
    args = [o_attn, u, u, u, jnp.asarray(lay.pos), jnp.asarray(lay.rem), pool_w_bf,
            row1(pool_scale), row1(g_attn), row1(g_pool), w_out_bf, h, row1(ln_ffn)]
    out_specs = [pl.BlockSpec((tm, D_MODEL), lambda i: (i, 0)),
                 pl.BlockSpec((tm, D_MODEL), lambda i: (i, 0))]
    out_shape = [jax.ShapeDtypeStruct((rows, D_MODEL), F32),
                 jax.ShapeDtypeStruct((rows, D_MODEL), F32 if moe else BF16)]
    scratch = [pltpu.VMEM((tm + 2 * POOL_HALO, POOL_WIDTH), F32),
               pltpu.VMEM((tm + 2 * POOL_HALO, POOL_WIDTH), BF16),
               pltpu.VMEM((tm + 2 * POOL_HALO, POOL_WIDTH), BF16)]
    if moe:
        tri = jnp.asarray(np.triu(np.ones((tm, tm), np.float32)), BF16)
        in_specs += [pl.BlockSpec((32, D_MODEL), lambda i: (0, 0)),
                     pl.BlockSpec((tm, tm), lambda i: (0, 0))]
        args += [router_t, tri]
        out_specs += [pl.BlockSpec((N_EXPERTS, tm), lambda i: (0, i)),
                      pl.BlockSpec((N_EXPERTS, LANES), lambda i: (0, 0))]
        out_shape += [jax.ShapeDtypeStruct((N_EXPERTS, rows), F32),
                      jax.ShapeDtypeStruct((N_EXPERTS, LANES), F32)]
        scratch += [pltpu.VMEM((N_EXPERTS, LANES), F32)]
    return pl.pallas_call(
        functools.partial(_mix_out_kernel, moe=moe, n_valid=lay.n_valid),
        grid=(rows // tm,),
        in_specs=in_specs,
        out_specs=out_specs,
        out_shape=out_shape,
        scratch_shapes=scratch,
        compiler_params=pltpu.CompilerParams(
            dimension_semantics=("arbitrary",), vmem_limit_bytes=VMEM_LIMIT),
        name="mix_out_moe" if moe else "mix_out",
    )(*args)


def _swiglu_step(x, wg_ref, wu_ref, wd_ref):
    gate = _dot(x, wg_ref[...])
    up = _dot(x, wu_ref[...])
    mid = (gate / (1.0 + jnp.exp(-gate))) * up
    return _dot(mid.astype(BF16), wd_ref[...])


def _ffn_dense_kernel(x_ref, wg_ref, wu_ref, wd_ref, res_ref, o_ref):
    @pl.when(pl.program_id(1) == 0)
    def _():
        o_ref[...] = res_ref[...]

    o_ref[...] += _swiglu_step(x_ref[...], wg_ref, wu_ref, wd_ref)


def _ffn_dense(x, wg_bf, wu_bf, wd_bf, res):
    rows = x.shape[0]
    tm, tf = TM_FFN, TF_FFN
    return pl.pallas_call(
        _ffn_dense_kernel,
        grid=(rows // tm, D_FF // tf),
        in_specs=[
            pl.BlockSpec((tm, D_MODEL), lambda i, j: (i, 0)),
            pl.BlockSpec((D_MODEL, tf), lambda i, j: (0, j)),
            pl.BlockSpec((D_MODEL, tf), lambda i, j: (0, j)),
            pl.BlockSpec((tf, D_MODEL), lambda i, j: (j, 0)),
            pl.BlockSpec((tm, D_MODEL), lambda i, j: (i, 0)),
        ],
        out_specs=pl.BlockSpec((tm, D_MODEL), lambda i, j: (i, 0)),
        out_shape=jax.ShapeDtypeStruct((rows, D_MODEL), F32),
        compiler_params=pltpu.CompilerParams(
            dimension_semantics=("arbitrary", "arbitrary"), vmem_limit_bytes=VMEM_LIMIT),
        name="ffn_dense",
    )(x, wg_bf, wu_bf, wd_bf, res)


def _ffn_moe_kernel(te_ref, nu_ref, tv_ref, src_ref, hn_ref, wg_ref, wu_ref, wd_ref, o_ref,
                    xbuf_ref, xb_ref, sem):
    del te_ref
    tm = o_ref.shape[0]
    n_tiles = pl.num_programs(0)
    nf = pl.num_programs(1)
    share = xbuf_ref.shape[1] // D_FF_STEPS
    i = pl.program_id(0)
    j = pl.program_id(1)
    n_used = nu_ref[0]
    used = i < n_used
    slot = lax.rem(i, 2)

    def row_copy(tile, row, buf):
        return pltpu.make_async_copy(hn_ref.at[pl.ds(src_ref[tile * tm + row], 1)],
                                     xbuf_ref.at[buf, pl.ds(row, 1)], sem.at[buf])

    def tile_rows(buf):
        return pltpu.make_async_copy(hn_ref.at[pl.ds(0, xbuf_ref.shape[1])], xbuf_ref.at[buf],
                                     sem.at[buf])

    @pl.when(j == 0)
    def _():
        @pl.when(i == 0)
        def _():
            def body(r, c):
                row_copy(0, r, 0).start()
                return c

            lax.fori_loop(0, xbuf_ref.shape[1], body, 0, unroll=8)

        @pl.when(i <= n_used)
        def _():
            tile_rows(slot).wait()

        @pl.when(used)
        def _():
            xb_ref[...] = xbuf_ref[slot, 0:tm, :].astype(BF16)

        o_ref[...] = jnp.zeros_like(o_ref)

    half = tv_ref[i] <= tm // 2

    @pl.when(jnp.logical_and(used, jnp.logical_not(half)))
    def _():
        for k in range(share):
            row_copy(i + 1, j * share + k, 1 - slot).start()
        o_ref[...] += _swiglu_step(xb_ref[...], wg_ref, wu_ref, wd_ref)

    @pl.when(jnp.logical_and(used, half))
    def _():
        for k in range(share):
            row_copy(i + 1, j * share + k, 1 - slot).start()
        o_ref[0:tm // 2, :] += _swiglu_step(xb_ref[0:tm // 2, :], wg_ref, wu_ref, wd_ref)

    @pl.when(jnp.logical_and(used, jnp.logical_and(i == n_tiles - 1, j == nf - 1)))
    def _():
        tile_rows(1 - slot).wait()


def _ffn_moe(hn, src, wg_bf, wu_bf, wd_bf, tile_expert, n_used, tile_valid, lay):
    tm, tf = TM_FFN, TF_FFN
    nf = D_FF // tf

    def jj(i, j, nu):
        return jnp.where(i < nu[0], j, nf - 1)

    grid_spec = pltpu.PrefetchScalarGridSpec(
        num_scalar_prefetch=4,
        grid=(lay.slots // tm, nf),
        in_specs=[
            pl.BlockSpec(memory_space=pl.ANY),
            pl.BlockSpec((None, D_MODEL, tf), lambda i, j, te, nu, tv, sr: (te[i], 0, jj(i, j, nu))),
            pl.BlockSpec((None, D_MODEL, tf), lambda i, j, te, nu, tv, sr: (te[i], 0, jj(i, j, nu))),
            pl.BlockSpec((None, tf, D_MODEL), lambda i, j, te, nu, tv, sr: (te[i], jj(i, j, nu), 0)),
        ],
        out_specs=pl.BlockSpec((tm, D_MODEL), lambda i, j, te, nu, tv, sr: (i, 0)),
        scratch_shapes=[
            pltpu.VMEM((2, GATHER_ROWS, D_MODEL), F32),
            pltpu.VMEM((tm, D_MODEL), BF16),
            pltpu.SemaphoreType.DMA((2,)),
        ],
    )
    return pl.pallas_call(
        _ffn_moe_kernel,
        grid_spec=grid_spec,
        out_shape=jax.ShapeDtypeStruct((lay.slots, D_MODEL), F32),
        compiler_params=pltpu.CompilerParams(
            dimension_semantics=("arbitrary", "arbitrary"), vmem_limit_bytes=VMEM_LIMIT),
        name="ffn_moe",
    )(tile_expert, n_used, tile_valid, src, hn, wg_bf, wu_bf, wd_bf)


def _slot_src_kernel(pos_ref, src_ref, *, td, slots):
    i = pl.program_id(0)

    @pl.when(i == 0)
    def _():
        def init(s, c):
            src_ref[s] = 0
            return c

        lax.fori_loop(0, slots, init, 0, unroll=8)

    base = i * td

    def body(r, c):
        src_ref[pos_ref[0, r]] = base + r
        src_ref[pos_ref[1, r]] = base + r
        return c

    lax.fori_loop(0, td, body, 0, unroll=8)


def _slot_sources(pos3, lay):
    td = lay.td
    return pl.pallas_call(
        functools.partial(_slot_src_kernel, td=td, slots=lay.src_len),
        grid=(lay.n_valid // td,),
        in_specs=[pl.BlockSpec((None, 2, td), lambda i: (i, 0, 0), memory_space=pltpu.SMEM)],
        out_specs=pl.BlockSpec(memory_space=pltpu.SMEM),
        out_shape=jax.ShapeDtypeStruct((lay.src_len,), jnp.int32),
        compiler_params=pltpu.CompilerParams(dimension_semantics=("arbitrary",)),
        name="slot_sources",
    )(pos3)


def _combine_kernel(pos_ref, nxt_ref, gate_ref, h_ref, gf_ref, y_ref, o_ref, buf_ref, sem, *, tc):
    t = pl.program_id(0)
    slot = lax.rem(t, 2)

    def start_rows(idx_ref, buf):
        def issue(r, c):
            for k in range(2):
                pltpu.make_async_copy(y_ref.at[pl.ds(idx_ref[k, r], 1)],
                                      buf_ref.at[buf, k, pl.ds(r, 1)], sem.at[buf]).start()
            return c

        lax.fori_loop(0, tc, issue, 0, unroll=8)

    @pl.when(t == 0)
    def _():
        start_rows(pos_ref, 0)

    @pl.when(t + 1 < pl.num_programs(0))
    def _():
        start_rows(nxt_ref, 1 - slot)

    for k in range(2):
        pltpu.make_async_copy(y_ref.at[pl.ds(0, tc)], buf_ref.at[slot, k], sem.at[slot]).wait()
    gate = gate_ref[...]
    y = gate[:, 0:1] * buf_ref[slot, 0] + gate[:, 1:2] * buf_ref[slot, 1]
    o_ref[...] = _rms(h_ref[...] + y, gf_ref[...]).astype(o_ref.dtype)


def _combine(h1, y_sorted, pos, gates, g_final, row0, batch, tokens):
    tc = TM_OUT
    per = tokens // tc
    el = pl.Element
    seq = N_META + tokens

    def grid_rows(a):
        a = a[:, row0:row0 + batch * seq].reshape(a.shape[0], batch, seq)[:, :, N_META:]
        return a.reshape(a.shape[0], batch * tokens)

    pos_g = grid_rows(pos).reshape(2, batch * per, tc).transpose(1, 0, 2)
    gates_g = grid_rows(gates).T

    def rmap(t):
        return (pl.multiple_of(row0 + N_META + (t // per) * seq + (t % per) * tc, ROW_MULT), 0)

    return pl.pallas_call(
        functools.partial(_combine_kernel, tc=tc),
        grid=(batch * per,),
        in_specs=[
            pl.BlockSpec((None, 2, tc), lambda t: (t, 0, 0), memory_space=pltpu.SMEM),
            pl.BlockSpec((None, 2, tc), lambda t: (jnp.minimum(t + 1, batch * per - 1), 0, 0),
                         memory_space=pltpu.SMEM),
            pl.BlockSpec((tc, 2), lambda t: (t, 0)),
            pl.BlockSpec((el(tc), el(D_MODEL)), rmap),
            pl.BlockSpec((1, D_MODEL), lambda t: (0, 0)),
            pl.BlockSpec(memory_space=pl.ANY),
        ],
        out_specs=pl.BlockSpec((None, tc, D_MODEL), lambda t: (t // per, t % per, 0)),
        out_shape=jax.ShapeDtypeStruct((batch, tokens, D_MODEL), F32),
        scratch_shapes=[pltpu.VMEM((2, 2, tc, D_MODEL), F32), pltpu.SemaphoreType.DMA((2,))],
        compiler_params=pltpu.CompilerParams(
            dimension_semantics=("arbitrary",), vmem_limit_bytes=VMEM_LIMIT),
        name="combine",
    )(pos_g, pos_g, gates_g, h1, g_final.reshape(1, D_MODEL), y_sorted)


def _routing_tables(route, cnt, lay):
    tmx = TM_FFN
    counts = cnt[:, 0].astype(jnp.int32)
    padded = (counts + tmx - 1) // tmx * tmx
    ends = jnp.cumsum(padded)
    offs = ends - padded
    e_idx = route[0:2].astype(jnp.int32)
    rank = route[4:6].astype(jnp.int32)
    onehot = e_idx[:, :, None] == jnp.arange(N_EXPERTS, dtype=jnp.int32)
    pos = jnp.sum(jnp.where(onehot, offs, 0), axis=-1) + rank
    pos = jnp.where(jnp.arange(pos.shape[1])[None, :] < lay.n_valid, pos, 0)
    n_tiles = lay.slots // tmx
    tile_start = jnp.arange(n_tiles, dtype=jnp.int32) * tmx
    tile_expert = jnp.minimum(jnp.sum(tile_start[:, None] >= ends[None, :], axis=1),
                              N_EXPERTS - 1).astype(jnp.int32)
    n_used = (ends[-1] // tmx).reshape(1).astype(jnp.int32)
    in_seg = tile_start - jnp.sum(jnp.where(tile_expert[:, None] == jnp.arange(N_EXPERTS), offs, 0), axis=1)
    tile_valid = jnp.clip(jnp.sum(jnp.where(tile_expert[:, None] == jnp.arange(N_EXPERTS), counts, 0),
                                  axis=1) - in_seg, 0, tmx).astype(jnp.int32)
    return pos, route[2:4], tile_expert, n_used, tile_valid


def _forward(xs, meta_tokens, ln_mix, w_in, rpb, meta_bias, pool_w, pool_scale, g_attn_out,
             g_pool_out, w_out, ln_ffn, ffn_w_gate, ffn_w_up, ffn_w_down, router, moe_w_gate,
             moe_w_up, moe_w_down, g_final):
    groups = [(x.shape[0], x.shape[1]) for x in xs]
    lay = _Layout(groups)
    depth = ln_mix.shape[0]
    assert depth == 2

    parts = []
    for x in xs:
        for b in range(x.shape[0]):
            parts += [meta_tokens.astype(F32), x[b]]
    parts.append(jnp.zeros((lay.rows - lay.n_valid, D_MODEL), F32))
    h = jnp.concatenate(parts, axis=0)

    outs = None
    for layer in range(depth):
        qkv, u = _in_proj(h, ln_mix[layer], w_in[layer].astype(BF16))
        bias8, mbias = _attn_bias_tables(rpb[layer], meta_bias[layer])
        o_attn = jnp.zeros((lay.rows, ATTN_WIDTH), BF16)
        o_attn = _attention_grid(qkv, o_attn, bias8, mbias, lay)
        o_attn = _attention_meta(qkv, o_attn, meta_bias[layer].astype(F32), lay)
        mix_args = (o_attn, u, h, lay, pool_w[layer].astype(BF16), pool_scale[layer],
                    g_attn_out[layer], g_pool_out[layer], w_out[layer].astype(BF16), ln_ffn[layer])
        i = layer // 2
        if layer % 2 == 0:
            h1, hn = _mix_out(*mix_args)
            h = _ffn_dense(hn, ffn_w_gate[i].astype(BF16), ffn_w_up[i].astype(BF16),
                           ffn_w_down[i].astype(BF16), h1)
        else:
            rt = router[i].T
            rt_hi = rt.astype(BF16)
            rt_lo = (rt - rt_hi.astype(F32)).astype(BF16)
            zpad = jnp.zeros_like(rt_hi)
            router_t = jnp.concatenate([rt_hi, zpad, rt_lo, zpad], axis=0)
            h1, hn, route, cnt = _mix_out(*mix_args, router_t=router_t)
            pos, gates, tile_expert, n_used, tile_valid = _routing_tables(route, cnt, lay)
            td = lay.td
            pos3 = pos[:, :lay.n_valid].reshape(2, lay.n_valid // td, td).transpose(1, 0, 2)
            src = _slot_sources(pos3, lay)
            y_sorted = _ffn_moe(hn, src, moe_w_gate[i].astype(BF16), moe_w_up[i].astype(BF16),
                                moe_w_down[i].astype(BF16), tile_expert, n_used, tile_valid, lay)
            outs = []
            seq = 0
            for b, t in groups:
                outs.append(_combine(h1, y_sorted, pos, gates, g_final, lay.seq_start[seq], b, t))
                seq += b
    return tuple(outs)


def kernel(x_prompt, x_sample, meta_tokens, ln_mix, w_in, rpb, meta_bias, pool_w, pool_scale,
           g_attn_out, g_pool_out, w_out, ln_ffn, ffn_w_gate, ffn_w_up, ffn_w_down, router,
           moe_w_gate, moe_w_up, moe_w_down, g_final):
    return _forward((x_prompt, x_sample), meta_tokens, ln_mix, w_in, rpb, meta_bias, pool_w,
                    pool_scale, g_attn_out, g_pool_out, w_out, ln_ffn, ffn_w_gate, ffn_w_up,
                    ffn_w_down, router, moe_w_gate, moe_w_up, moe_w_down, g_final)
```
